```python
import jax, jax.numpy as jnp
from jax import lax
import numpy as np

D_MODEL = 2048
BATCH = 4
SEQ = 4096
DEPTH = 2

N_MIXERS = 2
N_LAYERS_A = (DEPTH + 1) // 2
N_LAYERS_B = DEPTH // 2

H_A = 8
QK_W = D_MODEL // 2
V_W = D_MODEL
DQK = QK_W // H_A
DV = V_W // H_A
CHUNK = 128
IN_A = 2 * QK_W + 2 * V_W + 2 * H_A

D_RNN = D_MODEL
N_BLOCKS = 8
BLOCK_W = D_RNN // N_BLOCKS
CONV_W = 4
RG_C = 8.0

D_FF = 4 * D_MODEL

EPS = 1e-6

kernel_name = "hybrid_mlstm_rglru_adaln_trunk"


def rms_norm(x):
    xf = x.astype(jnp.float32)
    return (xf * lax.rsqrt(jnp.mean(xf * xf, axis=-1, keepdims=True) + EPS)).astype(x.dtype)


def ada_modulate(x, c, w, b):
    mod = jax.nn.silu(c) @ w + b
    shift, scale, gate = jnp.split(mod, 3, axis=-1)
    h = rms_norm(x) * (1.0 + scale[:, None, :]) + shift[:, None, :]
    return h, gate[:, None, :]


def mlstm_chunkwise(q, k, v, li, lf):
    B, H, S, _ = q.shape
    nc = S // CHUNK

    def chunks(t):
        return jnp.moveaxis(t.reshape(B, H, nc, CHUNK, *t.shape[3:]), 2, 0)

    tri = jnp.tril(jnp.ones((CHUNK, CHUNK), dtype=bool))

    def step(carry, xs):
        C, n, m = carry
        qc, kc, vc, lic, lfc = xs
        b = jnp.cumsum(lfc, axis=-1)
        d = b[..., :, None] - b[..., None, :] + lic[..., None, :]
        d = jnp.where(tri, d, -jnp.inf)
        inter = b + m[..., None]
        m_t = jnp.maximum(inter, jnp.max(d, axis=-1))
        w_inter = jnp.exp(inter - m_t)
        p = jnp.exp(d - m_t[..., None]) * jnp.einsum("bhtd,bhsd->bhts", qc, kc)
        num = (w_inter[..., None] * jnp.einsum("bhtd,bhde->bhte", qc, C)
               + jnp.einsum("bhts,bhse->bhte", p, vc))
        den = w_inter * jnp.einsum("bhtd,bhd->bht", qc, n) + jnp.sum(p, axis=-1)
        h = num / jnp.maximum(jnp.abs(den), jnp.exp(-m_t))[..., None]
        b_last = b[..., -1]
        g = b_last[..., None] - b + lic
        m_new = jnp.maximum(b_last + m, jnp.max(g, axis=-1))
        wk = jnp.exp(g - m_new[..., None])
        decay = jnp.exp(b_last + m - m_new)
        C_new = decay[..., None, None] * C + jnp.einsum("bhs,bhsd,bhse->bhde", wk, kc, vc)
        n_new = decay[..., None] * n + jnp.einsum("bhs,bhsd->bhd", wk, kc)
        return (C_new, n_new, m_new), h

    init = (jnp.zeros((B, H, DQK, DV), jnp.float32),
            jnp.zeros((B, H, DQK), jnp.float32),
            jnp.zeros((B, H), jnp.float32))
    _, hs = lax.scan(step, init, (chunks(q), chunks(k), chunks(v), chunks(li), chunks(lf)))
    return jnp.moveaxis(hs, 0, 2).reshape(B, H, S, DV)


def mlstm_mixer(h, w_in, b_gate, norm_g, w_out):
    B, S, _ = h.shape
    proj = h @ w_in
    q, k, v, o, ig, fg = jnp.split(
        proj, [QK_W, 2 * QK_W, 2 * QK_W + V_W, 2 * QK_W + 2 * V_W, 2 * QK_W + 2 * V_W + H_A], axis=-1)
    q = q.reshape(B, S, H_A, DQK).transpose(0, 2, 1, 3).astype(jnp.float32) * (DQK ** -0.5)
    k = k.reshape(B, S, H_A, DQK).transpose(0, 2, 1, 3).astype(jnp.float32)
    v = v.reshape(B, S, H_A, DV).transpose(0, 2, 1, 3).astype(jnp.float32)
    li = (ig.astype(jnp.float32) + b_gate[0].astype(jnp.float32)).transpose(0, 2, 1)
    lf = jax.nn.log_sigmoid(fg.astype(jnp.float32) + b_gate[1].astype(jnp.float32)).transpose(0, 2, 1)
    hh = mlstm_chunkwise(q, k, v, li, lf).transpose(0, 2, 1, 3)
    hh = hh * lax.rsqrt(jnp.mean(hh * hh, axis=-1, keepdims=True) + EPS)
    hh = hh.reshape(B, S, V_W) * norm_g.astype(jnp.float32)
    return (hh.astype(h.dtype) * jax.nn.sigmoid(o)) @ w_out


def _lin_combine(e1, e2):
    a1, b1 = e1
    a2, b2 = e2
    return a1 * a2, a2 * b1 + b2


def rglru_mixer(h, w_in, conv_w, conv_b, w_ra, b_ra, w_ri, b_ri, lam, w_out):
    B, S, _ = h.shape
    xb, gb = jnp.split(h @ w_in, 2, axis=-1)
    xb = lax.conv_general_dilated(
        xb, conv_w[:, None, :], window_strides=(1,), padding=[(CONV_W - 1, 0)],
        dimension_numbers=("NWC", "WIO", "NWC"), feature_group_count=D_RNN) + conv_b
    xf = xb.astype(jnp.float32)
    xblk = xf.reshape(B, S, N_BLOCKS, BLOCK_W)
    r = jax.nn.sigmoid(jnp.einsum("bsnc,ncd->bsnd", xblk, w_ra.astype(jnp.float32)).reshape(B, S, D_RNN)
                       + b_ra.astype(jnp.float32))
    i = jax.nn.sigmoid(jnp.einsum("bsnc,ncd->bsnd", xblk, w_ri.astype(jnp.float32)).reshape(B, S, D_RNN)
                       + b_ri.astype(jnp.float32))
    log_a = -RG_C * r * jax.nn.softplus(-lam.astype(jnp.float32))
    a = jnp.exp(log_a)
    u = jnp.sqrt(-jnp.expm1(2.0 * log_a)) * (i * xf)
    _, hs = lax.associative_scan(_lin_combine, (a, u), axis=1)
    y = hs.astype(h.dtype) * jax.nn.gelu(gb)
    return y @ w_out


def sq_relu_mlp(h, w1, w2):
    return jnp.square(jax.nn.relu(h @ w1)) @ w2


def setup_inputs(seed: int = 0) -> dict:
    key = jax.random.key(seed)
    ks = jax.random.split(key, 20)
    f32 = jnp.float32
    nrm = lambda k, shape, scale: jax.random.normal(k, shape, f32) * scale
    s_rg = jax.random.uniform(ks[16], (N_LAYERS_B, D_RNN), f32, 0.9, 0.999) ** (1.0 / RG_C)
    return {
        "x": nrm(ks[0], (BATCH, SEQ, D_MODEL), 1.0),
        "c": nrm(ks[1], (BATCH, D_MODEL), 1.0),
        "ada_w": nrm(ks[2], (DEPTH, 2, D_MODEL, 3 * D_MODEL), 0.5 * D_MODEL ** -0.5),
        "ada_b": nrm(ks[3], (DEPTH, 2, 3 * D_MODEL), 0.02),
        "a_w_in": nrm(ks[4], (N_LAYERS_A, D_MODEL, IN_A), D_MODEL ** -0.5),
        "a_b_gate": jnp.stack([
            nrm(ks[5], (N_LAYERS_A, H_A), 0.1),
            3.0 + 3.0 * jax.random.uniform(ks[6], (N_LAYERS_A, H_A), f32)], axis=1),
        "a_norm_g": 1.0 + nrm(ks[7], (N_LAYERS_A, V_W), 0.02),
        "a_w_out": nrm(ks[8], (N_LAYERS_A, V_W, D_MODEL), V_W ** -0.5),
        "b_w_in": nrm(ks[9], (N_LAYERS_B, D_MODEL, 2 * D_RNN), D_MODEL ** -0.5),
        "b_conv_w": nrm(ks[10], (N_LAYERS_B, CONV_W, D_RNN), CONV_W ** -0.5),
        "b_conv_b": nrm(ks[11], (N_LAYERS_B, D_RNN), 0.02),
        "b_w_ra": nrm(ks[12], (N_LAYERS_B, N_BLOCKS, BLOCK_W, BLOCK_W), BLOCK_W ** -0.5),
        "b_b_ra": nrm(ks[13], (N_LAYERS_B, D_RNN), 0.02),
        "b_w_ri": nrm(ks[14], (N_LAYERS_B, N_BLOCKS, BLOCK_W, BLOCK_W), BLOCK_W ** -0.5),
        "b_b_ri": nrm(ks[15], (N_LAYERS_B, D_RNN), 0.02),
        "b_lam": jnp.log(s_rg / (1.0 - s_rg)),
        "b_w_out": nrm(ks[17], (N_LAYERS_B, D_RNN, D_MODEL), D_RNN ** -0.5),
        "mlp_w1": nrm(ks[18], (DEPTH, D_MODEL, D_FF), D_MODEL ** -0.5),
        "mlp_w2": nrm(ks[19], (DEPTH, D_FF, D_MODEL), D_FF ** -0.5),
        "final_g": 1.0 + nrm(jax.random.fold_in(key, 99), (D_MODEL,), 0.02),
    }


def reference(x, c, ada_w, ada_b, a_w_in, a_b_gate, a_norm_g, a_w_out,
              b_w_in, b_conv_w, b_conv_b, b_w_ra, b_b_ra, b_w_ri, b_b_ri, b_lam, b_w_out,
              mlp_w1, mlp_w2, final_g):
    for layer in range(DEPTH):
        slot = layer // N_MIXERS
        h, gate = ada_modulate(x, c, ada_w[layer, 0], ada_b[layer, 0])
        if layer % N_MIXERS == 0:
            y = mlstm_mixer(h, a_w_in[slot], a_b_gate[slot], a_norm_g[slot], a_w_out[slot])
        else:
            y = rglru_mixer(h, b_w_in[slot], b_conv_w[slot], b_conv_b[slot], b_w_ra[slot], b_b_ra[slot],
                            b_w_ri[slot], b_b_ri[slot], b_lam[slot], b_w_out[slot])
        x = x + gate * y
        h, gate = ada_modulate(x, c, ada_w[layer, 1], ada_b[layer, 1])
        x = x + gate * sq_relu_mlp(h, mlp_w1[layer], mlp_w2[layer])
    return rms_norm(x) * final_g
```

```python
import functools

import jax
import jax.numpy as jnp
from jax import lax
from jax.experimental import pallas as pl
from jax.experimental.pallas import tpu as pltpu

EPS = 1e-6
RG_C = 8.0
CONV_W = 4
LSTM_CHUNK = 128
V7X_LANES = 128
V7X_SUBLANES = 8
VMEM_LIMIT = 56 * 1024 * 1024

F32 = jnp.float32
BF16 = jnp.bfloat16


def _params(sem):
    return pltpu.CompilerParams(dimension_semantics=sem, vmem_limit_bytes=VMEM_LIMIT)


def _mod_norm(x, scale, shift):
    ms = jnp.mean(x * x, axis=-1, keepdims=True)
    return (x * lax.rsqrt(ms + EPS)) * (1.0 + scale) + shift


def _ada_kernel(c_ref, w_ref, b_ref, o_ref):
    c = c_ref[...]
    s = (c * jax.nn.sigmoid(c)).astype(BF16)
    o_ref[0] = jnp.dot(s, w_ref[0].astype(BF16), preferred_element_type=F32) + b_ref[0]


def _ada_mod(c, ada_w, ada_b, tn=1024):
    depth, two, d, n = ada_w.shape
    sites = depth * two
    b = c.shape[0]
    bp = -(-b // V7X_SUBLANES) * V7X_SUBLANES
    c_pad = jnp.pad(c, ((0, bp - b), (0, 0)))
    w = ada_w.reshape(sites, d, n)
    bias = ada_b.reshape(sites, 1, n)
    out = pl.pallas_call(
        _ada_kernel,
        grid=(sites, n // tn),
        in_specs=[
            pl.BlockSpec((bp, d), lambda s, j: (0, 0)),
            pl.BlockSpec((1, d, tn), lambda s, j: (s, 0, j)),
            pl.BlockSpec((1, 1, tn), lambda s, j: (s, 0, j)),
        ],
        out_specs=pl.BlockSpec((1, bp, tn), lambda s, j: (s, 0, j)),
        out_shape=jax.ShapeDtypeStruct((sites, bp, n), F32),
        compiler_params=_params(("parallel", "parallel")),
        name="ada_mod",
    )(c_pad, w, bias)
    return out[:, :b, :]


def _norm_matmul_kernel(x_ref, sc_ref, sh_ref, w_ref, o_ref, h_ref):
    @pl.when(pl.program_id(2) == 0)
    def _():
        h_ref[...] = _mod_norm(x_ref[0], sc_ref[0], sh_ref[0]).astype(BF16)

    o_ref[0] = jnp.dot(h_ref[...], w_ref[...], preferred_element_type=F32).astype(o_ref.dtype)


def _norm_matmul_gates_kernel(x_ref, sc_ref, sh_ref, w_ref, wg_ref, o_ref, g_ref, h_ref):
    @pl.when(pl.program_id(2) == 0)
    def _():
        h = _mod_norm(x_ref[0], sc_ref[0], sh_ref[0]).astype(BF16)
        h_ref[...] = h
        g_ref[0] = jnp.dot(h, wg_ref[...], preferred_element_type=F32)

    o_ref[0] = jnp.dot(h_ref[...], w_ref[...], preferred_element_type=F32).astype(o_ref.dtype)


def _norm_matmul(x, scale, shift, w, w_gates=None, tm=1024, tn=1024):
    b, s, d = x.shape
    n = w.shape[1]
    grid = (b, s // tm, n // tn)
    x_spec = pl.BlockSpec((1, tm, d), lambda bi, i, j: (bi, i, 0))
    mod_spec = pl.BlockSpec((1, 1, d), lambda bi, i, j: (bi, 0, 0))
    w_spec = pl.BlockSpec((d, tn), lambda bi, i, j: (0, j))
    o_spec = pl.BlockSpec((1, tm, tn), lambda bi, i, j: (bi, i, j))
    scratch = [pltpu.VMEM((tm, d), BF16)]
    cp = _params(("parallel", "parallel", "arbitrary"))
    if w_gates is None:
        return pl.pallas_call(
            _norm_matmul_kernel, grid=grid,
            in_specs=[x_spec, mod_spec, mod_spec, w_spec],
            out_specs=o_spec,
            out_shape=jax.ShapeDtypeStruct((b, s, n), BF16),
            scratch_shapes=scratch, compiler_params=cp, name="norm_matmul",
        )(x, scale, shift, w)
    ng = w_gates.shape[1]
    return pl.pallas_call(
        _norm_matmul_gates_kernel, grid=grid,
        in_specs=[x_spec, mod_spec, mod_spec, w_spec,
                  pl.BlockSpec((d, ng), lambda bi, i, j: (0, 0))],
        out_specs=[o_spec, pl.BlockSpec((1, tm, ng), lambda bi, i, j: (bi, i, 0))],
        out_shape=[jax.ShapeDtypeStruct((b, s, n), BF16),
                   jax.ShapeDtypeStruct((b, s, ng), F32)],
        scratch_shapes=scratch, compiler_params=cp, name="norm_matmul_gates",
    )(x, scale, shift, w, w_gates)


def _out_proj_kernel(y_ref, w_ref, x_ref, g_ref, o_ref):
    acc = jnp.dot(y_ref[0], w_ref[...], preferred_element_type=F32)
    o_ref[0] = x_ref[0] + g_ref[0] * acc


def _out_proj_residual(y, w, x, gate, tm=1024, tn=1024):
    b, s, k = y.shape
    n = w.shape[1]
    return pl.pallas_call(
        _out_proj_kernel, grid=(b, s // tm, n // tn),
        in_specs=[
            pl.BlockSpec((1, tm, k), lambda bi, i, j: (bi, i, 0)),
            pl.BlockSpec((k, tn), lambda bi, i, j: (0, j)),
            pl.BlockSpec((1, tm, tn), lambda bi, i, j: (bi, i, j)),
            pl.BlockSpec((1, 1, tn), lambda bi, i, j: (bi, 0, j)),
        ],
        out_specs=pl.BlockSpec((1, tm, tn), lambda bi, i, j: (bi, i, j)),
        out_shape=jax.ShapeDtypeStruct((b, s, n), F32),
        compiler_params=_params(("parallel", "parallel", "arbitrary")),
        name="out_proj_residual",
    )(y, w, x, gate)


def _mlp_kernel(x_ref, sc_ref, sh_ref, g_ref, w1_ref, w2_ref, fg_ref, o_ref, h_ref, acc_ref,
                *, final_norm):
    f = pl.program_id(2)

    @pl.when(f == 0)
    def _():
        h_ref[...] = _mod_norm(x_ref[0], sc_ref[0], sh_ref[0]).astype(BF16)

    u = jnp.dot(h_ref[...], w1_ref[...], preferred_element_type=F32)
    u = jnp.square(jnp.maximum(u, 0.0)).astype(BF16)
    part = jnp.dot(u, w2_ref[...], preferred_element_type=F32)

    @pl.when(f == 0)
    def _():
        acc_ref[...] = part

    @pl.when(f > 0)
    def _():
        acc_ref[...] += part

    @pl.when(f == pl.num_programs(2) - 1)
    def _():
        y = x_ref[0] + g_ref[0] * acc_ref[...]
        if final_norm:
            ms = jnp.mean(y * y, axis=-1, keepdims=True)
            y = (y * lax.rsqrt(ms + EPS)) * fg_ref[...]
        o_ref[0] = y


def _mlp(x, scale, shift, gate, w1, w2, final_g, final_norm, tm=512, tf=1024):
    b, s, d = x.shape
    dff = w1.shape[1]
    mod_spec = pl.BlockSpec((1, 1, d), lambda bi, i, f: (bi, 0, 0))
    return pl.pallas_call(
        functools.partial(_mlp_kernel, final_norm=final_norm),
        grid=(b, s // tm, dff // tf),
        in_specs=[
            pl.BlockSpec((1, tm, d), lambda bi, i, f: (bi, i, 0)),
            mod_spec, mod_spec, mod_spec,
            pl.BlockSpec((d, tf), lambda bi, i, f: (0, f)),
            pl.BlockSpec((tf, d), lambda bi, i, f: (f, 0)),
            pl.BlockSpec((1, d), lambda bi, i, f: (0, 0)),
        ],
        out_specs=pl.BlockSpec((1, tm, d), lambda bi, i, f: (bi, i, 0)),
        out_shape=jax.ShapeDtypeStruct((b, s, d), F32),
        scratch_shapes=[pltpu.VMEM((tm, d), BF16), pltpu.VMEM((tm, d), F32)],
        compiler_params=_params(("parallel", "parallel", "arbitrary")),
        name="mlp_final" if final_norm else "mlp",
    )(x, scale, shift, gate, w1, w2, final_g)


def _split3_dot(tri_lhs, x, tri_rhs):
    x1 = x.astype(BF16)
    r1 = x - x1.astype(F32)
    x2 = r1.astype(BF16)
    x3 = (r1 - x2.astype(F32)).astype(BF16)
    if tri_lhs is not None:
        d = lambda t: jnp.dot(tri_lhs, t, preferred_element_type=F32)
    else:
        d = lambda t: jnp.dot(t, tri_rhs, preferred_element_type=F32)
    return d(x1) + d(x2) + d(x3)


def _log_sigmoid(z):
    return jnp.minimum(z, 0.0) - jnp.log1p(jnp.exp(-jnp.abs(z)))


def _mlstm_kernel(q_ref, k_ref, v_ref, o_ref, g_ref, bias_ref, ng_ref, y_ref,
                  c_ref, n_ref, m_ref, *, heads, dqk, dv):
    L = LSTM_CHUNK

    @pl.when(pl.program_id(1) == 0)
    def _():
        c_ref[...] = jnp.zeros_like(c_ref)
        n_ref[...] = jnp.zeros_like(n_ref)
        m_ref[...] = jnp.zeros_like(m_ref)

    row = lax.broadcasted_iota(jnp.int32, (L, L), 0)
    col = lax.broadcasted_iota(jnp.int32, (L, L), 1)
    causal = row >= col
    tril = jnp.where(causal, 1.0, 0.0).astype(BF16)
    triu = jnp.where(row <= col, 1.0, 0.0).astype(BF16)

    a_col = g_ref[0] + bias_ref[...]
    a_row = a_col.T
    b_col_all = _split3_dot(tril, _log_sigmoid(a_col), None)
    b_row_all = _split3_dot(None, _log_sigmoid(a_row), triu)

    scale = dqk ** -0.5
    for h in range(heads):
        li_col = a_col[:, h:h + 1]
        li_row = a_row[h:h + 1, :]
        b_col = b_col_all[:, heads + h:heads + h + 1]
        b_row = b_row_all[heads + h:heads + h + 1, :]
        m = m_ref[h:h + 1, 0:1]
        n_row = n_ref[h:h + 1, :]
        c_state = c_ref[h]

        q = q_ref[0, :, h * dqk:(h + 1) * dqk]
        k = k_ref[0, :, h * dqk:(h + 1) * dqk]
        v = v_ref[0, :, h * dv:(h + 1) * dv]

        d = jnp.where(causal, b_col - b_row + li_row, -jnp.inf)
        inter = b_col + m
        m_t = jnp.maximum(inter, jnp.max(d, axis=1, keepdims=True))
        w_inter = jnp.exp(inter - m_t)
        s = lax.dot_general(q, k, (((1,), (1,)), ((), ())), preferred_element_type=F32) * scale
        p = jnp.exp(d - m_t) * s
        qc = jnp.dot(q, c_state.astype(BF16), preferred_element_type=F32) * scale
        num = w_inter * qc + jnp.dot(p.astype(BF16), v, preferred_element_type=F32)
        qn = jnp.sum(q.astype(F32) * n_row, axis=1, keepdims=True) * scale
        den = w_inter * qn + jnp.sum(p, axis=1, keepdims=True)
        hh = num / jnp.maximum(jnp.abs(den), jnp.exp(-m_t))

        ms = jnp.mean(hh * hh, axis=1, keepdims=True)
        hn = (hh * lax.rsqrt(ms + EPS)) * ng_ref[:, h * dv:(h + 1) * dv]
        og = jax.nn.sigmoid(o_ref[0, :, h * dv:(h + 1) * dv].astype(F32))
        y_ref[0, :, h * dv:(h + 1) * dv] = (hn * og).astype(y_ref.dtype)

        b_last = b_col[L - 1:L, :]
        g_row = b_last - b_row + li_row
        m_new = jnp.maximum(b_last + m, jnp.max(g_row, axis=1, keepdims=True))
        decay = jnp.exp(b_last + m - m_new)
        wk_col = jnp.exp(b_last - b_col + li_col - m_new)
        kw = k.astype(F32) * wk_col
        kv = lax.dot_general(kw.astype(BF16), v, (((0,), (0,)), ((), ())),
                             preferred_element_type=F32)
        c_ref[h] = decay * c_state + kv
        n_ref[h:h + 1, :] = decay * n_row + jnp.sum(kw, axis=0, keepdims=True)
        m_ref[h:h + 1, :] = jnp.broadcast_to(m_new, (1, m_ref.shape[1]))


def _mlstm(proj, gates, gate_bias, norm_g, heads, dqk, dv):
    b, s, _ = proj.shape
    L = LSTM_CHUNK
    qk_w, v_w = heads * dqk, heads * dv
    assert v_w == 2 * qk_w
    kernel = functools.partial(_mlstm_kernel, heads=heads, dqk=dqk, dv=dv)
    return pl.pallas_call(
        kernel, grid=(b, s // L),
        in_specs=[
            pl.BlockSpec((1, L, qk_w), lambda bi, c: (bi, c, 0)),
            pl.BlockSpec((1, L, qk_w), lambda bi, c: (bi, c, 1)),
            pl.BlockSpec((1, L, v_w), lambda bi, c: (bi, c, 1)),
            pl.BlockSpec((1, L, v_w), lambda bi, c: (bi, c, 2)),
            pl.BlockSpec((1, L, gates.shape[2]), lambda bi, c: (bi, c, 0)),
            pl.BlockSpec((1, gate_bias.shape[1]), lambda bi, c: (0, 0)),
            pl.BlockSpec((1, v_w), lambda bi, c: (0, 0)),
        ],
        out_specs=pl.BlockSpec((1, L, v_w), lambda bi, c: (bi, c, 0)),
        out_shape=jax.ShapeDtypeStruct((b, s, v_w), BF16),
        scratch_shapes=[pltpu.VMEM((heads, dqk, dv), F32),
                        pltpu.VMEM((heads, dqk), F32),
                        pltpu.VMEM((heads, V7X_LANES), F32)],
        compiler_params=_params(("parallel", "arbitrary")),
        name="mlstm",
    )(proj, proj, proj, proj, gates, gate_bias, norm_g)


def _rglru_kernel(xb_ref, gb_ref, cw_ref, cb_ref, wra_ref, bra_ref, wri_ref, bri_ref, lam_ref,
                  y_ref, xs_ref, a_ref, u_ref, p_ref, hs_ref, carry_ref, *, tt, nblk, bw):
    S8, LN = V7X_SUBLANES, V7X_LANES
    seg = tt // S8

    @pl.when(pl.program_id(1) == 0)
    def _():
        xs_ref[0:S8, :] = jnp.zeros((S8, xs_ref.shape[1]), F32)
        carry_ref[...] = jnp.zeros_like(carry_ref)

    xs_ref[S8:S8 + tt, :] = xb_ref[0].astype(F32)
    xc = cb_ref[...] + cw_ref[CONV_W - 1:CONV_W, :] * xs_ref[S8:S8 + tt, :]
    for j in range(CONV_W - 1):
        off = S8 - (CONV_W - 1) + j
        xc = xc + cw_ref[j:j + 1, :] * xs_ref[off:off + tt, :]
    xs_ref[0:S8, :] = xs_ref[tt:tt + S8, :]

    lam = lam_ref[...]
    neg_softplus = -RG_C * (jnp.maximum(-lam, 0.0) + jnp.log1p(jnp.exp(-jnp.abs(lam))))
    for nb in range(nblk):
        sl = slice(nb * bw, (nb + 1) * bw)
        xn = xc[:, sl]
        xn16 = xn.astype(BF16)
        r = jax.nn.sigmoid(jnp.dot(xn16, wra_ref[nb], preferred_element_type=F32) + bra_ref[:, sl])
        i = jax.nn.sigmoid(jnp.dot(xn16, wri_ref[nb], preferred_element_type=F32) + bri_ref[:, sl])
        log_a = r * neg_softplus[:, sl]
        a = jnp.exp(log_a)
        one_m_a2 = -jnp.tanh(log_a) * (1.0 + a * a)
        u = jnp.sqrt(one_m_a2) * (i * xn)
        for lb in range(bw // LN):
            a_ref[nb * (bw // LN) + lb] = a[:, lb * LN:(lb + 1) * LN]
            u_ref[nb * (bw // LN) + lb] = u[:, lb * LN:(lb + 1) * LN]

    nlb = a_ref.shape[0]

    def step(j, hp):
        hs, ps = [], []
        for lb in range(nlb):
            aj = a_ref[lb, pl.ds(j, S8, stride=seg), :]
            uj = u_ref[lb, pl.ds(j, S8, stride=seg), :]
            h = aj * hp[0][lb] + uj
            p = aj * hp[1][lb]
            hs_ref[lb, pl.ds(j, S8, stride=seg), :] = h
            p_ref[lb, pl.ds(j, S8, stride=seg), :] = p
            hs.append(h)
            ps.append(p)
        return tuple(hs), tuple(ps)

    init = (tuple(jnp.zeros((S8, LN), F32) for _ in range(nlb)),
            tuple(jnp.ones((S8, LN), F32) for _ in range(nlb)))
    h_end, p_end = lax.fori_loop(0, seg, step, init)

    for lb in range(nlb):
        lanes = slice(lb * LN, (lb + 1) * LN)
        c = carry_ref[lb, 0:1, :]
        for sgi in range(S8):
            rows = slice(sgi * seg, (sgi + 1) * seg)
            hs = hs_ref[lb, rows, :] + p_ref[lb, rows, :] * c
            gate = jax.nn.gelu(gb_ref[0, rows, lanes].astype(F32), approximate=True)
            y_ref[0, rows, lanes] = (hs * gate).astype(y_ref.dtype)
            c = h_end[lb][sgi:sgi + 1, :] + p_end[lb][sgi:sgi + 1, :] * c
        carry_ref[lb, 0:1, :] = c


def _rglru(proj, conv_w, conv_b, w_ra, b_ra, w_ri, b_ri, lam, tt=256):
    b, s, two_d = proj.shape
    d = two_d // 2
    nblk, bw, _ = w_ra.shape
    row = lambda: pl.BlockSpec((1, d), lambda bi, i: (0, 0))
    wblk = lambda: pl.BlockSpec((nblk, bw, bw), lambda bi, i: (0, 0, 0))
    kernel = functools.partial(_rglru_kernel, tt=tt, nblk=nblk, bw=bw)
    lane_blocked = (d // V7X_LANES, tt, V7X_LANES)
    return pl.pallas_call(
        kernel, grid=(b, s // tt),
        in_specs=[
            pl.BlockSpec((1, tt, d), lambda bi, i: (bi, i, 0)),
            pl.BlockSpec((1, tt, d), lambda bi, i: (bi, i, 1)),
            pl.BlockSpec((CONV_W, d), lambda bi, i: (0, 0)),
            row(), wblk(), row(), wblk(), row(), row(),
        ],
        out_specs=pl.BlockSpec((1, tt, d), lambda bi, i: (bi, i, 0)),
        out_shape=jax.ShapeDtypeStruct((b, s, d), BF16),
        scratch_shapes=[pltpu.VMEM((tt + V7X_SUBLANES, d), F32),
                        pltpu.VMEM(lane_blocked, F32),
                        pltpu.VMEM(lane_blocked, F32),
                        pltpu.VMEM(lane_blocked, F32),
                        pltpu.VMEM(lane_blocked, F32),
                        pltpu.VMEM((d // V7X_LANES, V7X_SUBLANES, V7X_LANES), F32)],
        compiler_params=_params(("parallel", "arbitrary")),
        name="rglru",
    )(proj, proj, conv_w, conv_b, w_ra, b_ra, w_ri, b_ri, lam)


def kernel(x, c, ada_w, ada_b, a_w_in, a_b_gate, a_norm_g, a_w_out, b_w_in, b_conv_w, b_conv_b,
           b_w_ra, b_b_ra, b_w_ri, b_b_ri, b_lam, b_w_out, mlp_w1, mlp_w2, final_g):
    depth = ada_w.shape[0]
    d = x.shape[-1]
    heads = a_b_gate.shape[-1]
    v_w = a_norm_g.shape[-1]
    qk_w = (a_w_in.shape[-1] - 2 * v_w - 2 * heads) // 2
    dqk, dv = qk_w // heads, v_w // heads
    n_main = 2 * qk_w + 2 * v_w

    mod = _ada_mod(c, ada_w, ada_b)
    final_row = final_g.reshape(1, d)

    def site(layer, j):
        m = mod[layer * 2 + j]
        return (m[:, None, d:2 * d], m[:, None, 0:d], m[:, None, 2 * d:3 * d])

    for layer in range(depth):
        slot = layer // 2
        scale, shift, gate = site(layer, 0)
        if layer % 2 == 0:
            w_in = a_w_in[slot]
            w_gates = jnp.pad(w_in[:, n_main:], ((0, 0), (0, V7X_LANES - 2 * heads))).astype(BF16)
            proj, gates = _norm_matmul(x, scale, shift, w_in[:, :n_main].astype(BF16), w_gates)
            gate_bias = jnp.pad(a_b_gate[slot].reshape(1, 2 * heads).astype(F32),
                                ((0, 0), (0, V7X_LANES - 2 * heads)))
            y = _mlstm(proj, gates, gate_bias, a_norm_g[slot].reshape(1, v_w).astype(F32),
                       heads, dqk, dv)
            w_out = a_w_out[slot]
        else:
            proj = _norm_matmul(x, scale, shift, b_w_in[slot].astype(BF16))
            drnn = b_lam.shape[-1]
            y = _rglru(proj, b_conv_w[slot], b_conv_b[slot].reshape(1, drnn),
                       b_w_ra[slot].astype(BF16), b_b_ra[slot].reshape(1, drnn),
                       b_w_ri[slot].astype(BF16), b_b_ri[slot].reshape(1, drnn),
                       b_lam[slot].reshape(1, drnn))
            w_out = b_w_out[slot]
        x = _out_proj_residual(y, w_out.astype(BF16), x, gate)
        scale, shift, gate = site(layer, 1)
        x = _mlp(x, scale, shift, gate, mlp_w1[layer].astype(BF16), mlp_w2[layer].astype(BF16),
                 final_row, final_norm=(layer == depth - 1))
    return x
```

```python
import functools

import jax
import jax.numpy as jnp
from jax import lax
from jax.experimental import pallas as pl
from jax.experimental.pallas import tpu as pltpu

EPS = 1e-6
RG_C = 8.0
CONV_W = 4
LSTM_CHUNK = 128
V7X_LANES = 128
V7X_SUBLANES = 8
VMEM_LIMIT = 56 * 1024 * 1024

F32 = jnp.float32
BF16 = jnp.bfloat16


def _params(sem):
    return pltpu.CompilerParams(dimension_semantics=sem, vmem_limit_bytes=VMEM_LIMIT)


NORM_ROWS = 16


def _mod_norm_to(x_ref, sc_ref, sh_ref, h_ref, rs_ref, copy_ref=None):
    d = x_ref.shape[2]
    nlb = d // V7X_LANES

    def stats(r, carry):
        rows = pl.ds(pl.multiple_of(r * NORM_ROWS, NORM_ROWS), NORM_ROWS)
        x = x_ref[0, rows, :]
        ms = jnp.mean(x * x, axis=-1, keepdims=True)
        rs_ref[rows, :] = jnp.broadcast_to(lax.rsqrt(ms + EPS), (NORM_ROWS, V7X_LANES))
        return carry

    def apply(r, carry):
        rows = pl.ds(pl.multiple_of(r * NORM_ROWS, NORM_ROWS), NORM_ROWS)
        rs = rs_ref[rows, :]
        for lb in range(nlb):
            lanes = slice(lb * V7X_LANES, (lb + 1) * V7X_LANES)
            x = x_ref[0, rows, lanes]
            h = (x * rs) * (1.0 + sc_ref[0, :, lanes]) + sh_ref[0, :, lanes]
            h_ref[rows, lanes] = h.astype(h_ref.dtype)
            if copy_ref is not None:
                copy_ref[0, rows, lanes] = x
        return carry

    trips = h_ref.shape[0] // NORM_ROWS
    lax.fori_loop(0, trips, stats, 0, unroll=8)
    lax.fori_loop(0, trips, apply, 0, unroll=2)


def _ada_kernel(c_ref, w_ref, b_ref, o_ref):
    c = c_ref[...]
    s = (c * jax.nn.sigmoid(c)).astype(BF16)
    o_ref[0] = jnp.dot(s, w_ref[0].astype(BF16), preferred_element_type=F32) + b_ref[0]


def _ada_mod(c, ada_w, ada_b, tn=1024):
    depth, two, d, n = ada_w.shape
    sites = depth * two
    b = c.shape[0]
    bp = -(-b // V7X_SUBLANES) * V7X_SUBLANES
    c_pad = jnp.pad(c, ((0, bp - b), (0, 0)))
    w = ada_w.reshape(sites, d, n)
    bias = ada_b.reshape(sites, 1, n)
    out = pl.pallas_call(
        _ada_kernel,
        grid=(sites, n // tn),
        in_specs=[
            pl.BlockSpec((bp, d), lambda s, j: (0, 0)),
            pl.BlockSpec((1, d, tn), lambda s, j: (s, 0, j)),
            pl.BlockSpec((1, 1, tn), lambda s, j: (s, 0, j)),
        ],
        out_specs=pl.BlockSpec((1, bp, tn), lambda s, j: (s, 0, j)),
        out_shape=jax.ShapeDtypeStruct((sites, bp, n), F32),
        compiler_params=_params(("parallel", "parallel")),
        name="ada_mod",
    )(c_pad, w, bias)
    return out[:, :b, :]


def _norm_matmul_kernel(x_ref, sc_ref, sh_ref, w_ref, o_ref, h_ref, rs_ref):
    @pl.when(pl.program_id(2) == 0)
    def _():
        _mod_norm_to(x_ref, sc_ref, sh_ref, h_ref, rs_ref)

    o_ref[0] = jnp.dot(h_ref[...], w_ref[...], preferred_element_type=F32).astype(o_ref.dtype)


def _norm_matmul_gates_kernel(x_ref, sc_ref, sh_ref, w_ref, wg_ref, o_ref, g_ref, h_ref, rs_ref):
    @pl.when(pl.program_id(2) == 0)
    def _():
        _mod_norm_to(x_ref, sc_ref, sh_ref, h_ref, rs_ref)
        g_ref[0] = jnp.dot(h_ref[...], wg_ref[...], preferred_element_type=F32)

    o_ref[0] = jnp.dot(h_ref[...], w_ref[...], preferred_element_type=F32).astype(o_ref.dtype)


def _norm_matmul(x, scale, shift, w, w_gates=None, tm=1024, tn=1024):
    b, s, d = x.shape
    n = w.shape[1]
    grid = (b, s // tm, n // tn)
    x_spec = pl.BlockSpec((1, tm, d), lambda bi, i, j: (bi, i, 0))
    mod_spec = pl.BlockSpec((1, 1, d), lambda bi, i, j: (bi, 0, 0))
    w_spec = pl.BlockSpec((d, tn), lambda bi, i, j: (0, j))
    o_spec = pl.BlockSpec((1, tm, tn), lambda bi, i, j: (bi, i, j))
    scratch = [pltpu.VMEM((tm, d), BF16), pltpu.VMEM((tm, V7X_LANES), F32)]
    cp = _params(("parallel", "parallel", "arbitrary"))
    if w_gates is None:
        return pl.pallas_call(
            _norm_matmul_kernel, grid=grid,
            in_specs=[x_spec, mod_spec, mod_spec, w_spec],
            out_specs=o_spec,
            out_shape=jax.ShapeDtypeStruct((b, s, n), BF16),
            scratch_shapes=scratch, compiler_params=cp, name="norm_matmul",
        )(x, scale, shift, w)
    ng = w_gates.shape[1]
    return pl.pallas_call(
        _norm_matmul_gates_kernel, grid=grid,
        in_specs=[x_spec, mod_spec, mod_spec, w_spec,
                  pl.BlockSpec((d, ng), lambda bi, i, j: (0, 0))],
        out_specs=[o_spec, pl.BlockSpec((1, tm, ng), lambda bi, i, j: (bi, i, 0))],
        out_shape=[jax.ShapeDtypeStruct((b, s, n), BF16),
                   jax.ShapeDtypeStruct((b, s, ng), F32)],
        scratch_shapes=scratch, compiler_params=cp, name="norm_matmul_gates",
    )(x, scale, shift, w, w_gates)


def _out_proj_kernel(y_ref, w_ref, x_ref, g_ref, o_ref):
    acc = jnp.dot(y_ref[0], w_ref[...], preferred_element_type=F32)
    o_ref[0] = x_ref[0] + g_ref[0] * acc


def _out_proj_residual(y, w, x, gate, tm=1024, tn=1024):
    b, s, k = y.shape
    n = w.shape[1]
    return pl.pallas_call(
        _out_proj_kernel, grid=(b, s // tm, n // tn),
        in_specs=[
            pl.BlockSpec((1, tm, k), lambda bi, i, j: (bi, i, 0)),
            pl.BlockSpec((k, tn), lambda bi, i, j: (0, j)),
            pl.BlockSpec((1, tm, tn), lambda bi, i, j: (bi, i, j)),
            pl.BlockSpec((1, 1, tn), lambda bi, i, j: (bi, 0, j)),
        ],
        out_specs=pl.BlockSpec((1, tm, tn), lambda bi, i, j: (bi, i, j)),
        out_shape=jax.ShapeDtypeStruct((b, s, n), F32),
        compiler_params=_params(("parallel", "parallel", "arbitrary")),
        name="out_proj_residual",
    )(y, w, x, gate)


def _mlp_kernel(x_ref, sc_ref, sh_ref, g_ref, w1_ref, w2_ref, fg_ref, o_ref, h_ref, rs_ref,
                *, final_norm):
    f = pl.program_id(2)

    @pl.when(f == 0)
    def _():
        _mod_norm_to(x_ref, sc_ref, sh_ref, h_ref, rs_ref, copy_ref=o_ref)

    u = jnp.dot(h_ref[...], w1_ref[...], preferred_element_type=F32)
    u = jnp.square(jnp.maximum(u, 0.0)).astype(BF16)
    o_ref[0] += g_ref[0] * jnp.dot(u, w2_ref[...], preferred_element_type=F32)

    if final_norm:
        @pl.when(f == pl.num_programs(2) - 1)
        def _():
            y = o_ref[0]
            ms = jnp.mean(y * y, axis=-1, keepdims=True)
            o_ref[0] = (y * lax.rsqrt(ms + EPS)) * fg_ref[...]


def _mlp(x, scale, shift, gate, w1, w2, final_g, final_norm, tm=1024, tf=512):
    b, s, d = x.shape
    dff = w1.shape[1]
    mod_spec = pl.BlockSpec((1, 1, d), lambda bi, i, f: (bi, 0, 0))
    return pl.pallas_call(
        functools.partial(_mlp_kernel, final_norm=final_norm),
        grid=(b, s // tm, dff // tf),
        in_specs=[
            pl.BlockSpec((1, tm, d), lambda bi, i, f: (bi, i, 0)),
            mod_spec, mod_spec, mod_spec,
            pl.BlockSpec((d, tf), lambda bi, i, f: (0, f)),
            pl.BlockSpec((tf, d), lambda bi, i, f: (f, 0)),
            pl.BlockSpec((1, d), lambda bi, i, f: (0, 0)),
        ],
        out_specs=pl.BlockSpec((1, tm, d), lambda bi, i, f: (bi, i, 0)),
        out_shape=jax.ShapeDtypeStruct((b, s, d), F32),
        scratch_shapes=[pltpu.VMEM((tm, d), BF16), pltpu.VMEM((tm, V7X_LANES), F32)],
        compiler_params=_params(("parallel", "parallel", "arbitrary")),
        name="mlp_final" if final_norm else "mlp",
    )(x, scale, shift, gate, w1, w2, final_g)


def _cumsum_rows(x, tril):
    x1 = x.astype(BF16)
    r1 = x - x1.astype(F32)
    x2 = r1.astype(BF16)
    x3 = (r1 - x2.astype(F32)).astype(BF16)
    d = lambda t: jnp.dot(tril, t, preferred_element_type=F32)
    return d(x1) + d(x2) + d(x3)


def _cummax_rows(x):
    n = x.shape[0]
    row = lax.broadcasted_iota(jnp.int32, x.shape, 0)
    sh = 1
    while sh < n:
        if sh < V7X_SUBLANES:
            prev = jnp.where(row >= sh, pltpu.roll(x, sh, 0), -jnp.inf)
        else:
            prev = jnp.concatenate([jnp.full((sh, x.shape[1]), -jnp.inf, x.dtype), x[:n - sh]], axis=0)
        x = jnp.maximum(x, prev)
        sh *= 2
    return x


def _log_sigmoid(z):
    return jnp.minimum(z, 0.0) - jnp.log1p(jnp.exp(-jnp.abs(z)))


def _mlstm_kernel(q_ref, k_ref, v_ref, o_ref, g_ref, bias_ref, ng_ref, y_ref, cn_ref, m_ref,
                  *, heads, dqk, dv):
    L, LN = LSTM_CHUNK, V7X_LANES

    @pl.when(pl.program_id(1) == 0)
    def _():
        cn_ref[...] = jnp.zeros_like(cn_ref)
        m_ref[...] = jnp.zeros_like(m_ref)

    row = lax.broadcasted_iota(jnp.int32, (L, L), 0)
    col = lax.broadcasted_iota(jnp.int32, (L, L), 1)
    causal = row >= col
    tril = jnp.where(causal, 1.0, 0.0).astype(BF16)
    eye = jnp.where(lax.broadcasted_iota(jnp.int32, (dqk, dqk), 0)
                    == lax.broadcasted_iota(jnp.int32, (dqk, dqk), 1), 1.0, 0.0).astype(BF16)

    log_in = g_ref[0, :, 0:LN] + bias_ref[:, 0:LN]
    b = _cumsum_rows(_log_sigmoid(g_ref[0, :, LN:2 * LN] + bias_ref[:, LN:2 * LN]), tril)
    r = log_in - b
    cm = _cummax_rows(r)
    m_row = m_ref[0:1, :]
    mx_last = jnp.maximum(m_row, cm[L - 1:L, :])
    decay_row = jnp.exp(m_row - mx_last)
    wk = jnp.exp(r - mx_last)
    m_ref[0:1, :] = b[L - 1:L, :] + mx_last
    r_rows = r.T
    wk_rows = wk.T

    floor_scale = float(dqk) ** 0.5
    ones_blk = jnp.ones((L, LN), BF16)
    nt = (((1,), (1,)), ((), ()))
    hr = range(heads)
    q = [q_ref[0, :, h * dqk:(h + 1) * dqk] for h in hr]
    k = [k_ref[0, :, h * dqk:(h + 1) * dqk] for h in hr]
    s = [lax.dot_general(q[h], k[h], nt, preferred_element_type=F32) for h in hr]
    k_t = [lax.dot_general(eye, k[h], nt, preferred_element_type=F32) for h in hr]

    lhs, rhs, vo, floor = [], [], [], []
    for h in hr:
        m_h = m_row[:, h:h + 1]
        mx = jnp.maximum(jnp.broadcast_to(cm[:, h:h + 1], (L, LN)), m_h)
        b_t = jnp.broadcast_to(b[:, h:h + 1], (L, LN))
        p = jnp.exp(jnp.where(causal, r_rows[h:h + 1, :] - mx, -jnp.inf)) * s[h]
        wq = (q[h].astype(F32) * jnp.exp(m_h - mx)).astype(BF16)
        lhs.append(jnp.concatenate([wq, p.astype(BF16)], axis=1))
        vo.append(jnp.concatenate([v_ref[0, :, h * dv:(h + 1) * dv], ones_blk], axis=1))
        rhs.append(jnp.concatenate([cn_ref[h].astype(BF16), vo[h]], axis=0))
        floor.append(jnp.exp(-(b_t + mx)) * floor_scale)

    out = [jnp.dot(lhs[h], rhs[h], preferred_element_type=F32) for h in hr]

    for h in hr:
        kw_t = (k_t[h] * wk_rows[h:h + 1, :]).astype(BF16)
        cn_ref[h] = (decay_row[:, h:h + 1] * cn_ref[h]
                     + jnp.dot(kw_t, vo[h], preferred_element_type=F32))

    for h in hr:
        inv = 1.0 / jnp.maximum(jnp.abs(out[h][:, dv:]), floor[h])
        hh = out[h][:, :dv] * jnp.concatenate([inv] * (dv // LN), axis=1)
        ms = jnp.mean(hh * hh, axis=1, keepdims=True)
        hn = (hh * lax.rsqrt(ms + EPS)) * ng_ref[:, h * dv:(h + 1) * dv]
        og = 0.5 * jnp.tanh(0.5 * o_ref[0, :, h * dv:(h + 1) * dv].astype(F32)) + 0.5
        y_ref[0, :, h * dv:(h + 1) * dv] = (hn * og).astype(y_ref.dtype)


def _mlstm(proj, gates, gate_bias, norm_g, heads, dqk, dv):
    b, s, _ = proj.shape
    L = LSTM_CHUNK
    qk_w, v_w = heads * dqk, heads * dv
    assert v_w == 2 * qk_w and dqk == L and heads <= V7X_LANES
    kernel = functools.partial(_mlstm_kernel, heads=heads, dqk=dqk, dv=dv)
    return pl.pallas_call(
        kernel, grid=(b, s // L),
        in_specs=[
            pl.BlockSpec((1, L, qk_w), lambda bi, c: (bi, c, 0)),
            pl.BlockSpec((1, L, qk_w), lambda bi, c: (bi, c, 1)),
            pl.BlockSpec((1, L, v_w), lambda bi, c: (bi, c, 1)),
            pl.BlockSpec((1, L, v_w), lambda bi, c: (bi, c, 2)),
            pl.BlockSpec((1, L, gates.shape[2]), lambda bi, c: (bi, c, 0)),
            pl.BlockSpec((1, gate_bias.shape[1]), lambda bi, c: (0, 0)),
            pl.BlockSpec((1, v_w), lambda bi, c: (0, 0)),
        ],
        out_specs=pl.BlockSpec((1, L, v_w), lambda bi, c: (bi, c, 0)),
        out_shape=jax.ShapeDtypeStruct((b, s, v_w), BF16),
        scratch_shapes=[pltpu.VMEM((heads, dqk, dv + V7X_LANES), F32),
                        pltpu.VMEM((V7X_SUBLANES, V7X_LANES), F32)],
        compiler_params=_params(("parallel", "arbitrary")),
        name="mlstm",
    )(proj, proj, proj, proj, gates, gate_bias, norm_g)


def _scan_permutation(tt):
    seg = tt // V7X_SUBLANES
    r = jnp.arange(tt)
    src = (r % V7X_SUBLANES) * seg + r // V7X_SUBLANES
    to_interleaved = (src[:, None] == jnp.arange(tt)[None, :]).astype(BF16)
    return to_interleaved, to_interleaved.T


def _rglru_kernel(xb_ref, gb_ref, pm_ref, pmt_ref, cw_ref, cb_ref, wra_ref, bra_ref, wri_ref, bri_ref,
                  lam_ref, y_ref, a_ref, u_ref, tail_ref, carry_ref, *, tt, nblk, bw):
    S8 = V7X_SUBLANES
    seg = tt // S8
    H = CONV_W - 1
    d = a_ref.shape[1]

    @pl.when(pl.program_id(1) == 0)
    def _():
        tail_ref[...] = jnp.zeros_like(tail_ref)
        carry_ref[...] = jnp.zeros_like(carry_ref)

    pm = pm_ref[...]
    lam = lam_ref[...]
    neg_softplus = -RG_C * (jnp.maximum(-lam, 0.0) + jnp.log1p(jnp.exp(-jnp.abs(lam))))
    first_segment = lax.broadcasted_iota(jnp.int32, (S8, bw), 0) == 0

    for nb in range(nblk):
        sl = slice(nb * bw, (nb + 1) * bw)
        xp = jnp.dot(pm, xb_ref[0, :, sl], preferred_element_type=F32)
        hist = []
        for g in range(H):
            cur = xp[(seg - H + g) * S8:(seg - H + g + 1) * S8, :]
            prev = tail_ref[g * S8:(g + 1) * S8, sl]
            hist.append(jnp.where(first_segment, pltpu.roll(prev, 1, 0), pltpu.roll(cur, 1, 0)))
        tail_ref[:, sl] = xp[(seg - H) * S8:, :]
        xext = jnp.concatenate(hist + [xp], axis=0)
        xc = cb_ref[:, sl] + cw_ref[H:H + 1, sl] * xp
        for k in range(H):
            xc = xc + cw_ref[k:k + 1, sl] * xext[k * S8:k * S8 + tt, :]

        xc16 = xc.astype(BF16)
        r = 0.5 * jnp.tanh(jnp.dot(xc16, wra_ref[nb], preferred_element_type=F32) + bra_ref[:, sl]) + 0.5
        i = 0.5 * jnp.tanh(jnp.dot(xc16, wri_ref[nb], preferred_element_type=F32) + bri_ref[:, sl]) + 0.5
        log_a = r * neg_softplus[:, sl]
        a = jnp.exp(log_a)
        one_m_a2 = -jnp.tanh(log_a) * (1.0 + a * a)
        a_ref[:, sl] = a
        u_ref[:, sl] = jnp.sqrt(one_m_a2) * (i * xc)

    def step(j, hp):
        h, p = hp
        rows = pl.ds(pl.multiple_of(j * S8, S8), S8)
        a = a_ref[rows, :]
        h = a * h + u_ref[rows, :]
        p = a * p
        u_ref[rows, :] = h
        a_ref[rows, :] = p
        return h, p

    h_end, p_end = lax.fori_loop(0, seg, step, (jnp.zeros((S8, d), F32), jnp.ones((S8, d), F32)),
                                 unroll=2)

    c = carry_ref[0:1, :]
    carry_rows = []
    for sgi in range(S8):
        carry_rows.append(c)
        c = h_end[sgi:sgi + 1, :] + p_end[sgi:sgi + 1, :] * c
    carry_ref[0:1, :] = c
    carry_in = jnp.concatenate(carry_rows, axis=0)

    pmt = pmt_ref[...]
    for nb in range(nblk):
        sl = slice(nb * bw, (nb + 1) * bw)
        cin = jnp.concatenate([carry_in[:, sl]] * seg, axis=0)
        hs = u_ref[:, sl] + a_ref[:, sl] * cin
        gp = jnp.dot(pm, gb_ref[0, :, sl], preferred_element_type=F32)
        yi = (hs * jax.nn.gelu(gp, approximate=True)).astype(BF16)
        y_ref[0, :, sl] = jnp.dot(pmt, yi, preferred_element_type=F32).astype(y_ref.dtype)


def _rglru(proj, conv_w, conv_b, w_ra, b_ra, w_ri, b_ri, lam, tt=256):
    b, s, two_d = proj.shape
    d = two_d // 2
    nblk, bw, _ = w_ra.shape
    pm, pmt = _scan_permutation(tt)
    const = lambda shape: pl.BlockSpec(shape, lambda bi, i: (0,) * len(shape))
    kernel = functools.partial(_rglru_kernel, tt=tt, nblk=nblk, bw=bw)
    return pl.pallas_call(
        kernel, grid=(b, s // tt),
        in_specs=[
            pl.BlockSpec((1, tt, d), lambda bi, i: (bi, i, 0)),
            pl.BlockSpec((1, tt, d), lambda bi, i: (bi, i, 1)),
            const((tt, tt)), const((tt, tt)),
            const((CONV_W, d)), const((1, d)),
            const((nblk, bw, bw)), const((1, d)), const((nblk, bw, bw)), const((1, d)),
            const((1, d)),
        ],
        out_specs=pl.BlockSpec((1, tt, d), lambda bi, i: (bi, i, 0)),
        out_shape=jax.ShapeDtypeStruct((b, s, d), BF16),
        scratch_shapes=[pltpu.VMEM((tt, d), F32),
                        pltpu.VMEM((tt, d), F32),
                        pltpu.VMEM(((CONV_W - 1) * V7X_SUBLANES, d), F32),
                        pltpu.VMEM((V7X_SUBLANES, d), F32)],
        compiler_params=_params(("parallel", "arbitrary")),
        name="rglru",
    )(proj, proj, pm, pmt, conv_w, conv_b, w_ra, b_ra, w_ri, b_ri, lam)


def _gate_lanes(ig, fg):
    pad = [(0, 0)] * (ig.ndim - 1) + [(0, V7X_LANES - ig.shape[-1])]
    return jnp.concatenate([jnp.pad(ig, pad), jnp.pad(fg, pad)], axis=-1)


def kernel(x, c, ada_w, ada_b, a_w_in, a_b_gate, a_norm_g, a_w_out, b_w_in, b_conv_w, b_conv_b,
           b_w_ra, b_b_ra, b_w_ri, b_b_ri, b_lam, b_w_out, mlp_w1, mlp_w2, final_g):
    depth = ada_w.shape[0]
    d = x.shape[-1]
    heads = a_b_gate.shape[-1]
    v_w = a_norm_g.shape[-1]
    qk_w = (a_w_in.shape[-1] - 2 * v_w - 2 * heads) // 2
    dqk, dv = qk_w // heads, v_w // heads
    n_main = 2 * qk_w + 2 * v_w

    mod = _ada_mod(c, ada_w, ada_b)
    final_row = final_g.reshape(1, d)

    def site(layer, j):
        m = mod[layer * 2 + j]
        return (m[:, None, d:2 * d], m[:, None, 0:d], m[:, None, 2 * d:3 * d])

    for layer in range(depth):
        slot = layer // 2
        scale, shift, gate = site(layer, 0)
        if layer % 2 == 0:
            w_in = a_w_in[slot]
            w_gates = _gate_lanes(w_in[:, n_main:n_main + heads], w_in[:, n_main + heads:]).astype(BF16)
            proj, gates = _norm_matmul(x, scale, shift, w_in[:, :n_main].astype(BF16), w_gates)
            gate_bias = _gate_lanes(a_b_gate[slot, 0:1].astype(F32), a_b_gate[slot, 1:2].astype(F32))
            y = _mlstm(proj, gates, gate_bias, a_norm_g[slot].reshape(1, v_w).astype(F32),
                       heads, dqk, dv)
            w_out = a_w_out[slot]
        else:
            proj = _norm_matmul(x, scale, shift, b_w_in[slot].astype(BF16))
            drnn = b_lam.shape[-1]
            y = _rglru(proj, b_conv_w[slot], b_conv_b[slot].reshape(1, drnn),
                       (0.5 * b_w_ra[slot]).astype(BF16), 0.5 * b_b_ra[slot].reshape(1, drnn),
                       (0.5 * b_w_ri[slot]).astype(BF16), 0.5 * b_b_ri[slot].reshape(1, drnn),
                       b_lam[slot].reshape(1, drnn))
            w_out = b_w_out[slot]
        x = _out_proj_residual(y, w_out.astype(BF16), x, gate)
        scale, shift, gate = site(layer, 1)
        x = _mlp(x, scale, shift, gate, mlp_w1[layer].astype(BF16), mlp_w2[layer].astype(BF16),
                 final_row, final_norm=(layer == depth - 1))
    return x
```

```python
import functools

import jax
import jax.numpy as jnp
from jax import lax
from jax.experimental import pallas as pl
from jax.experimental.pallas import tpu as pltpu

EPS = 1e-6
RG_C = 8.0
CONV_W = 4
LSTM_CHUNK = 128
V7X_LANES = 128
V7X_SUBLANES = 8
VMEM_LIMIT = 56 * 1024 * 1024

F32 = jnp.float32
BF16 = jnp.bfloat16


def _params(sem):
    return pltpu.CompilerParams(dimension_semantics=sem, vmem_limit_bytes=VMEM_LIMIT)


def _mod_norm(x, scale, shift):
    ms = jnp.mean(x * x, axis=-1, keepdims=True)
    return (x * lax.rsqrt(ms + EPS)) * (1.0 + scale) + shift


BF16_SUBLANES = 16


def _cast_block(w, n_steps):
    r, c = w.shape
    col_blocks = 1
    while r * col_blocks < n_steps * BF16_SUBLANES:
        col_blocks *= 2
    assert (r * col_blocks) % n_steps == 0 and c % (col_blocks * V7X_LANES) == 0, (w.shape, n_steps)
    return (r * col_blocks // n_steps, c // col_blocks), col_blocks


def _cast_specs(casts, n_steps, step_id):
    in_specs, out_specs, out_shapes = [], [], []
    for w in casts:
        block, cb = _cast_block(w, n_steps)
        index = lambda *ids, cb=cb: (step_id(*ids) // cb, step_id(*ids) % cb)
        in_specs.append(pl.BlockSpec(block, index))
        out_specs.append(pl.BlockSpec(block, index))
        out_shapes.append(jax.ShapeDtypeStruct(w.shape, BF16))
    return in_specs, out_specs, out_shapes


def _run_casts(src_refs, dst_refs):
    for src, dst in zip(src_refs, dst_refs):
        dst[...] = src[...].astype(dst.dtype)


def _ada_kernel(c_ref, w_ref, b_ref, o_ref):
    c = c_ref[...]
    s = (c * jax.nn.sigmoid(c)).astype(BF16)
    o_ref[0] = jnp.dot(s, w_ref[0].astype(BF16), preferred_element_type=F32) + b_ref[0]


def _ada_mod(c, ada_w, ada_b, tn=1024):
    depth, two, d, n = ada_w.shape
    sites = depth * two
    b = c.shape[0]
    bp = -(-b // V7X_SUBLANES) * V7X_SUBLANES
    c_pad = jnp.pad(c, ((0, bp - b), (0, 0)))
    w = ada_w.reshape(sites, d, n)
    bias = ada_b.reshape(sites, 1, n)
    out = pl.pallas_call(
        _ada_kernel,
        grid=(sites, n // tn),
        in_specs=[
            pl.BlockSpec((bp, d), lambda s, j: (0, 0)),
            pl.BlockSpec((1, d, tn), lambda s, j: (s, 0, j)),
            pl.BlockSpec((1, 1, tn), lambda s, j: (s, 0, j)),
        ],
        out_specs=pl.BlockSpec((1, bp, tn), lambda s, j: (s, 0, j)),
        out_shape=jax.ShapeDtypeStruct((sites, bp, n), F32),
        compiler_params=_params(("parallel", "parallel")),
        name="ada_mod",
    )(c_pad, w, bias)
    return out[:, :b, :]


def _norm_matmul_kernel(*refs, has_gates, n_cast, cast_j):
    n_in = 4 + has_gates
    x_ref, sc_ref, sh_ref, w_ref = refs[:4]
    cast_src = refs[n_in:n_in + n_cast]
    outs = refs[n_in + n_cast:-1]
    o_ref, cast_dst, h_ref = outs[0], outs[1 + has_gates:], refs[-1]
    j = pl.program_id(2)

    @pl.when(j == 0)
    def _():
        h = _mod_norm(x_ref[0], sc_ref[0], sh_ref[0]).astype(BF16)
        h_ref[...] = h
        if has_gates:
            outs[1][0] = jnp.dot(h, refs[4][...], preferred_element_type=F32)

    o_ref[0] = jnp.dot(h_ref[...], w_ref[...], preferred_element_type=F32).astype(o_ref.dtype)

    if n_cast:
        pl.when(j < cast_j)(lambda: _run_casts(cast_src, cast_dst))


def _norm_matmul(x, scale, shift, w, w_gates=None, casts=(), tm=1024, tn=1024):
    b, s, d = x.shape
    n = w.shape[1]
    nj = n // tn
    grid = (b, s // tm, nj)
    cast_j = 1 << (nj.bit_length() - 1)
    step_id = lambda bi, i, j: (bi * (s // tm) + i) * cast_j + jnp.minimum(j, cast_j - 1)
    c_in, c_out, c_shapes = _cast_specs(casts, b * (s // tm) * cast_j, step_id)
    mod_spec = pl.BlockSpec((1, 1, d), lambda bi, i, j: (bi, 0, 0))
    in_specs = [pl.BlockSpec((1, tm, d), lambda bi, i, j: (bi, i, 0)), mod_spec, mod_spec,
                pl.BlockSpec((d, tn), lambda bi, i, j: (0, j))]
    out_specs = [pl.BlockSpec((1, tm, tn), lambda bi, i, j: (bi, i, j))]
    out_shapes = [jax.ShapeDtypeStruct((b, s, n), BF16)]
    args = [x, scale, shift, w]
    if w_gates is not None:
        ng = w_gates.shape[1]
        in_specs.append(pl.BlockSpec((d, ng), lambda bi, i, j: (0, 0)))
        out_specs.append(pl.BlockSpec((1, tm, ng), lambda bi, i, j: (bi, i, 0)))
        out_shapes.append(jax.ShapeDtypeStruct((b, s, ng), F32))
        args.append(w_gates)
    kernel = functools.partial(_norm_matmul_kernel, has_gates=w_gates is not None,
                               n_cast=len(casts), cast_j=cast_j)
    return pl.pallas_call(
        kernel, grid=grid,
        in_specs=in_specs + c_in, out_specs=out_specs + c_out, out_shape=out_shapes + c_shapes,
        scratch_shapes=[pltpu.VMEM((tm, d), BF16)],
        compiler_params=_params(("parallel", "parallel", "arbitrary")),
        name="norm_matmul_gates" if w_gates is not None else "norm_matmul",
    )(*args, *casts)


def _out_proj_kernel(y_ref, w_ref, x_ref, g_ref, o_ref):
    acc = jnp.dot(y_ref[0], w_ref[...], preferred_element_type=F32)
    o_ref[0] = x_ref[0] + g_ref[0] * acc


def _out_proj_residual(y, w, x, gate, tm=1024, tn=1024):
    b, s, k = y.shape
    n = w.shape[1]
    return pl.pallas_call(
        _out_proj_kernel, grid=(b, s // tm, n // tn),
        in_specs=[
            pl.BlockSpec((1, tm, k), lambda bi, i, j: (bi, i, 0)),
            pl.BlockSpec((k, tn), lambda bi, i, j: (0, j)),
            pl.BlockSpec((1, tm, tn), lambda bi, i, j: (bi, i, j)),
            pl.BlockSpec((1, 1, tn), lambda bi, i, j: (bi, 0, j)),
        ],
        out_specs=pl.BlockSpec((1, tm, tn), lambda bi, i, j: (bi, i, j)),
        out_shape=jax.ShapeDtypeStruct((b, s, n), F32),
        compiler_params=_params(("parallel", "parallel", "arbitrary")),
        name="out_proj_residual",
    )(y, w, x, gate)


def _mlp_kernel(*refs, final_norm, n_cast):
    x_ref, sc_ref, sh_ref, g_ref, w1_ref, w2_ref, fg_ref = refs[:7]
    cast_src = refs[7:7 + n_cast]
    o_ref, cast_dst, h_ref = refs[7 + n_cast], refs[8 + n_cast:-1], refs[-1]
    f = pl.program_id(2)

    @pl.when(f == 0)
    def _():
        x = x_ref[0]
        h_ref[...] = _mod_norm(x, sc_ref[0], sh_ref[0]).astype(BF16)
        o_ref[0] = x

    u = jnp.dot(h_ref[...], w1_ref[...], preferred_element_type=F32)
    u = jnp.square(jnp.maximum(u, 0.0)).astype(BF16)
    o_ref[0] += g_ref[0] * jnp.dot(u, w2_ref[...], preferred_element_type=F32)
    _run_casts(cast_src, cast_dst)

    if final_norm:
        @pl.when(f == pl.num_programs(2) - 1)
        def _():
            y = o_ref[0]
            ms = jnp.mean(y * y, axis=-1, keepdims=True)
            o_ref[0] = (y * lax.rsqrt(ms + EPS)) * fg_ref[...]


def _mlp(x, scale, shift, gate, w1, w2, final_g, final_norm, casts=(), tm=1024, tf=512):
    b, s, d = x.shape
    dff = w1.shape[1]
    ni, nf = s // tm, dff // tf
    c_in, c_out, c_shapes = _cast_specs(casts, b * ni * nf, lambda bi, i, f: (bi * ni + i) * nf + f)
    mod_spec = pl.BlockSpec((1, 1, d), lambda bi, i, f: (bi, 0, 0))
    out = pl.pallas_call(
        functools.partial(_mlp_kernel, final_norm=final_norm, n_cast=len(casts)),
        grid=(b, ni, nf),
        in_specs=[
            pl.BlockSpec((1, tm, d), lambda bi, i, f: (bi, i, 0)),
            mod_spec, mod_spec, mod_spec,
            pl.BlockSpec((d, tf), lambda bi, i, f: (0, f)),
            pl.BlockSpec((tf, d), lambda bi, i, f: (f, 0)),
            pl.BlockSpec((1, d), lambda bi, i, f: (0, 0)),
        ] + c_in,
        out_specs=[pl.BlockSpec((1, tm, d), lambda bi, i, f: (bi, i, 0))] + c_out,
        out_shape=[jax.ShapeDtypeStruct((b, s, d), F32)] + c_shapes,
        scratch_shapes=[pltpu.VMEM((tm, d), BF16)],
        compiler_params=_params(("parallel", "parallel", "arbitrary")),
        name="mlp_final" if final_norm else "mlp",
    )(x, scale, shift, gate, w1, w2, final_g, *casts)
    return out[0], out[1:]


def _cumsum_rows(x, tril):
    x1 = x.astype(BF16)
    r1 = x - x1.astype(F32)
    x2 = r1.astype(BF16)
    x3 = (r1 - x2.astype(F32)).astype(BF16)
    d = lambda t: jnp.dot(tril, t, preferred_element_type=F32)
    return d(x1) + d(x2) + d(x3)


def _cummax_rows(x):
    n = x.shape[0]
    row = lax.broadcasted_iota(jnp.int32, x.shape, 0)
    sh = 1
    while sh < n:
        if sh < V7X_SUBLANES:
            prev = jnp.where(row >= sh, pltpu.roll(x, sh, 0), -jnp.inf)
        else:
            prev = jnp.concatenate([jnp.full((sh, x.shape[1]), -jnp.inf, x.dtype), x[:n - sh]], axis=0)
        x = jnp.maximum(x, prev)
        sh *= 2
    return x


def _log_sigmoid(z):
    return jnp.minimum(z, 0.0) - jnp.log1p(jnp.exp(-jnp.abs(z)))


def _mlstm_kernel(q_ref, k_ref, v_ref, o_ref, g_ref, bias_ref, ng_ref, y_ref, cn_ref, m_ref,
                  *, heads, dqk, dv):
    L, LN = LSTM_CHUNK, V7X_LANES

    @pl.when(pl.program_id(1) == 0)
    def _():
        cn_ref[...] = jnp.zeros_like(cn_ref)
        m_ref[...] = jnp.zeros_like(m_ref)

    row = lax.broadcasted_iota(jnp.int32, (L, L), 0)
    col = lax.broadcasted_iota(jnp.int32, (L, L), 1)
    causal = row >= col
    tril = jnp.where(causal, 1.0, 0.0).astype(BF16)
    eye = jnp.where(lax.broadcasted_iota(jnp.int32, (dqk, dqk), 0)
                    == lax.broadcasted_iota(jnp.int32, (dqk, dqk), 1), 1.0, 0.0).astype(BF16)

    log_in = g_ref[0, :, 0:LN] + bias_ref[:, 0:LN]
    b = _cumsum_rows(_log_sigmoid(g_ref[0, :, LN:2 * LN] + bias_ref[:, LN:2 * LN]), tril)
    r = log_in - b
    cm = _cummax_rows(r)
    m_row = m_ref[0:1, :]
    mx_last = jnp.maximum(m_row, cm[L - 1:L, :])
    decay_row = jnp.exp(m_row - mx_last)
    wk = jnp.exp(r - mx_last)
    m_ref[0:1, :] = b[L - 1:L, :] + mx_last
    r_rows = r.T
    wk_rows = wk.T

    floor_scale = float(dqk) ** 0.5
    ones_blk = jnp.ones((L, LN), BF16)
    nt = (((1,), (1,)), ((), ()))
    hr = range(heads)
    q = [q_ref[0, :, h * dqk:(h + 1) * dqk] for h in hr]
    k = [k_ref[0, :, h * dqk:(h + 1) * dqk] for h in hr]
    s = [lax.dot_general(q[h], k[h], nt, preferred_element_type=F32) for h in hr]
    k_t = [lax.dot_general(eye, k[h], nt, preferred_element_type=F32) for h in hr]

    lhs, rhs, vo, floor = [], [], [], []
    for h in hr:
        m_h = m_row[:, h:h + 1]
        mx = jnp.maximum(jnp.broadcast_to(cm[:, h:h + 1], (L, LN)), m_h)
        b_t = jnp.broadcast_to(b[:, h:h + 1], (L, LN))
        p = jnp.exp(jnp.where(causal, r_rows[h:h + 1, :] - mx, -jnp.inf)) * s[h]
        wq = (q[h].astype(F32) * jnp.exp(m_h - mx)).astype(BF16)
        lhs.append(jnp.concatenate([wq, p.astype(BF16)], axis=1))
        vo.append(jnp.concatenate([v_ref[0, :, h * dv:(h + 1) * dv], ones_blk], axis=1))
        rhs.append(jnp.concatenate([cn_ref[h].astype(BF16), vo[h]], axis=0))
        floor.append(jnp.exp(-(b_t + mx)) * floor_scale)

    out = [jnp.dot(lhs[h], rhs[h], preferred_element_type=F32) for h in hr]

    for h in hr:
        kw_t = (k_t[h] * wk_rows[h:h + 1, :]).astype(BF16)
        cn_ref[h] = (decay_row[:, h:h + 1] * cn_ref[h]
                     + jnp.dot(kw_t, vo[h], preferred_element_type=F32))

    for h in hr:
        inv = 1.0 / jnp.maximum(jnp.abs(out[h][:, dv:]), floor[h])
        hh = out[h][:, :dv] * jnp.concatenate([inv] * (dv // LN), axis=1)
        ms = jnp.mean(hh * hh, axis=1, keepdims=True)
        hn = (hh * lax.rsqrt(ms + EPS)) * ng_ref[:, h * dv:(h + 1) * dv]
        og = 0.5 * jnp.tanh(0.5 * o_ref[0, :, h * dv:(h + 1) * dv].astype(F32)) + 0.5
        y_ref[0, :, h * dv:(h + 1) * dv] = (hn * og).astype(y_ref.dtype)


def _mlstm(proj, gates, gate_bias, norm_g, heads, dqk, dv):
    b, s, _ = proj.shape
    L = LSTM_CHUNK
    qk_w, v_w = heads * dqk, heads * dv
    assert v_w == 2 * qk_w and dqk == L and heads <= V7X_LANES
    kernel = functools.partial(_mlstm_kernel, heads=heads, dqk=dqk, dv=dv)
    return pl.pallas_call(
        kernel, grid=(b, s // L),
        in_specs=[
            pl.BlockSpec((1, L, qk_w), lambda bi, c: (bi, c, 0)),
            pl.BlockSpec((1, L, qk_w), lambda bi, c: (bi, c, 1)),
            pl.BlockSpec((1, L, v_w), lambda bi, c: (bi, c, 1)),
            pl.BlockSpec((1, L, v_w), lambda bi, c: (bi, c, 2)),
            pl.BlockSpec((1, L, gates.shape[2]), lambda bi, c: (bi, c, 0)),
            pl.BlockSpec((1, gate_bias.shape[1]), lambda bi, c: (0, 0)),
            pl.BlockSpec((1, v_w), lambda bi, c: (0, 0)),
        ],
        out_specs=pl.BlockSpec((1, L, v_w), lambda bi, c: (bi, c, 0)),
        out_shape=jax.ShapeDtypeStruct((b, s, v_w), BF16),
        scratch_shapes=[pltpu.VMEM((heads, dqk, dv + V7X_LANES), F32),
                        pltpu.VMEM((V7X_SUBLANES, V7X_LANES), F32)],
        compiler_params=_params(("parallel", "arbitrary")),
        name="mlstm",
    )(proj, proj, proj, proj, gates, gate_bias, norm_g)


def _scan_permutation(tt):
    seg = tt // V7X_SUBLANES
    r = jnp.arange(tt)
    src = (r % V7X_SUBLANES) * seg + r // V7X_SUBLANES
    to_interleaved = (src[:, None] == jnp.arange(tt)[None, :]).astype(BF16)
    return to_interleaved, to_interleaved.T


def _sqrt_nonneg(x):
    return jnp.where(x > 0.0, x * lax.rsqrt(x), 0.0)


def _gelu_tanh(x):
    c = (2.0 / jnp.pi) ** 0.5
    hx = 0.5 * x
    return hx + hx * jnp.tanh(x * (c + (c * 0.044715) * (x * x)))


def _rglru_kernel(xb_ref, gb_ref, pm_ref, pmt_ref, cw_ref, cb_ref, wra_ref, bra_ref, wri_ref, bri_ref,
                  lam_ref, y_ref, a_ref, u_ref, tail_ref, carry_ref, *, tt, nblk, bw):
    S8 = V7X_SUBLANES
    seg = tt // S8
    H = CONV_W - 1
    d = a_ref.shape[1]

    @pl.when(pl.program_id(1) == 0)
    def _():
        tail_ref[...] = jnp.zeros_like(tail_ref)
        carry_ref[...] = jnp.zeros_like(carry_ref)

    pm = pm_ref[...]
    lam = lam_ref[...]
    half_nsp = (-0.5 * RG_C) * (jnp.maximum(-lam, 0.0) + jnp.log1p(jnp.exp(-jnp.abs(lam))))
    first_segment = lax.broadcasted_iota(jnp.int32, (S8, bw), 0) == 0

    for nb in range(nblk):
        sl = slice(nb * bw, (nb + 1) * bw)
        xp = jnp.dot(pm, xb_ref[0, :, sl], preferred_element_type=F32)
        hist = []
        for g in range(H):
            cur = xp[(seg - H + g) * S8:(seg - H + g + 1) * S8, :]
            prev = tail_ref[g * S8:(g + 1) * S8, sl]
            hist.append(jnp.where(first_segment, pltpu.roll(prev, 1, 0), pltpu.roll(cur, 1, 0)))
        tail_ref[:, sl] = xp[(seg - H) * S8:, :]
        xext = jnp.concatenate(hist + [xp], axis=0)
        xc = cb_ref[:, sl] + cw_ref[H:H + 1, sl] * xp
        for k in range(H):
            xc = xc + cw_ref[k:k + 1, sl] * xext[k * S8:k * S8 + tt, :]

        xc16 = xc.astype(BF16)
        t_r = jnp.tanh(jnp.dot(xc16, wra_ref[nb], preferred_element_type=F32) + bra_ref[:, sl])
        i = 0.5 * jnp.tanh(jnp.dot(xc16, wri_ref[nb], preferred_element_type=F32) + bri_ref[:, sl]) + 0.5
        log_a = t_r * half_nsp[:, sl] + half_nsp[:, sl]
        a = jnp.exp(log_a)
        one_m_a2 = jnp.tanh(log_a) * (-1.0 - a * a)
        a_ref[:, sl] = a
        u_ref[:, sl] = _sqrt_nonneg(one_m_a2) * (i * xc)

    def step(j, hp):
        h, p = hp
        rows = pl.ds(pl.multiple_of(j * S8, S8), S8)
        a = a_ref[rows, :]
        h = a * h + u_ref[rows, :]
        p = a * p
        u_ref[rows, :] = h
        a_ref[rows, :] = p
        return h, p

    h_end, p_end = lax.fori_loop(0, seg, step, (jnp.zeros((S8, d), F32), jnp.ones((S8, d), F32)),
                                 unroll=2)

    c = carry_ref[0:1, :]
    carry_rows = []
    for sgi in range(S8):
        carry_rows.append(c)
        c = h_end[sgi:sgi + 1, :] + p_end[sgi:sgi + 1, :] * c
    carry_ref[0:1, :] = c
    carry_in = jnp.concatenate(carry_rows, axis=0)

    pmt = pmt_ref[...]
    for nb in range(nblk):
        sl = slice(nb * bw, (nb + 1) * bw)
        cin = jnp.concatenate([carry_in[:, sl]] * seg, axis=0)
        hs = u_ref[:, sl] + a_ref[:, sl] * cin
        gp = jnp.dot(pm, gb_ref[0, :, sl], preferred_element_type=F32)
        yi = (hs * _gelu_tanh(gp)).astype(BF16)
        y_ref[0, :, sl] = jnp.dot(pmt, yi, preferred_element_type=F32).astype(y_ref.dtype)


def _rglru(proj, conv_w, conv_b, w_ra, b_ra, w_ri, b_ri, lam, tt=256):
    b, s, two_d = proj.shape
    d = two_d // 2
    nblk, bw, _ = w_ra.shape
    pm, pmt = _scan_permutation(tt)
    const = lambda shape: pl.BlockSpec(shape, lambda bi, i: (0,) * len(shape))
    kernel = functools.partial(_rglru_kernel, tt=tt, nblk=nblk, bw=bw)
    return pl.pallas_call(
        kernel, grid=(b, s // tt),
        in_specs=[
            pl.BlockSpec((1, tt, d), lambda bi, i: (bi, i, 0)),
            pl.BlockSpec((1, tt, d), lambda bi, i: (bi, i, 1)),
            const((tt, tt)), const((tt, tt)),
            const((CONV_W, d)), const((1, d)),
            const((nblk, bw, bw)), const((1, d)), const((nblk, bw, bw)), const((1, d)),
            const((1, d)),
        ],
        out_specs=pl.BlockSpec((1, tt, d), lambda bi, i: (bi, i, 0)),
        out_shape=jax.ShapeDtypeStruct((b, s, d), BF16),
        scratch_shapes=[pltpu.VMEM((tt, d), F32),
                        pltpu.VMEM((tt, d), F32),
                        pltpu.VMEM(((CONV_W - 1) * V7X_SUBLANES, d), F32),
                        pltpu.VMEM((V7X_SUBLANES, d), F32)],
        compiler_params=_params(("parallel", "arbitrary")),
        name="rglru",
    )(proj, proj, pm, pmt, conv_w, conv_b, w_ra, b_ra, w_ri, b_ri, lam)


def _gate_lanes(ig, fg):
    pad = [(0, 0)] * (ig.ndim - 1) + [(0, V7X_LANES - ig.shape[-1])]
    return jnp.concatenate([jnp.pad(ig, pad), jnp.pad(fg, pad)], axis=-1)


def kernel(x, c, ada_w, ada_b, a_w_in, a_b_gate, a_norm_g, a_w_out, b_w_in, b_conv_w, b_conv_b,
           b_w_ra, b_b_ra, b_w_ri, b_b_ri, b_lam, b_w_out, mlp_w1, mlp_w2, final_g):
    depth = ada_w.shape[0]
    d = x.shape[-1]
    heads = a_b_gate.shape[-1]
    v_w = a_norm_g.shape[-1]
    qk_w = (a_w_in.shape[-1] - 2 * v_w - 2 * heads) // 2
    dqk, dv = qk_w // heads, v_w // heads
    n_main = 2 * qk_w + 2 * v_w

    mod = _ada_mod(c, ada_w, ada_b)
    final_row = final_g.reshape(1, d)

    def site(layer, j):
        m = mod[layer * 2 + j]
        return (m[:, None, d:2 * d], m[:, None, 0:d], m[:, None, 2 * d:3 * d])

    def big_weights(layer):
        slot = layer // 2
        if layer % 2 == 0:
            return a_w_in[slot][:, :n_main], a_w_out[slot], mlp_w1[layer], mlp_w2[layer]
        return b_w_in[slot], b_w_out[slot], mlp_w1[layer], mlp_w2[layer]

    w_in16 = big_weights(0)[0].astype(BF16)
    for layer in range(depth):
        slot = layer // 2
        casts = big_weights(layer)[1:] if layer == 0 else ()
        scale, shift, gate = site(layer, 0)
        if layer % 2 == 0:
            w_g = a_w_in[slot][:, n_main:]
            w_gates = _gate_lanes(w_g[:, :heads], w_g[:, heads:]).astype(BF16)
            proj, gates, *converted = _norm_matmul(x, scale, shift, w_in16, w_gates, casts=casts)
            gate_bias = _gate_lanes(a_b_gate[slot, 0:1].astype(F32), a_b_gate[slot, 1:2].astype(F32))
            y = _mlstm(proj, gates, gate_bias, a_norm_g[slot].reshape(1, v_w).astype(F32),
                       heads, dqk, dv)
        else:
            proj, *converted = _norm_matmul(x, scale, shift, w_in16, casts=casts)
            drnn = b_lam.shape[-1]
            y = _rglru(proj, b_conv_w[slot], b_conv_b[slot].reshape(1, drnn),
                       (0.5 * b_w_ra[slot]).astype(BF16), 0.5 * b_b_ra[slot].reshape(1, drnn),
                       (0.5 * b_w_ri[slot]).astype(BF16), 0.5 * b_b_ri[slot].reshape(1, drnn),
                       b_lam[slot].reshape(1, drnn))
        if converted:
            w_out16, w1_16, w2_16 = converted
        x = _out_proj_residual(y, w_out16, x, gate)
        scale, shift, gate = site(layer, 1)
        last = layer == depth - 1
        x, converted = _mlp(x, scale, shift, gate, w1_16, w2_16, final_row, final_norm=last,
                            casts=() if last else big_weights(layer + 1))
        if converted:
            w_in16, w_out16, w1_16, w2_16 = converted
    return x
```

```python
import functools

import jax
import jax.numpy as jnp
from jax import lax
from jax.experimental import pallas as pl
from jax.experimental.pallas import tpu as pltpu

EPS = 1e-6
RG_C = 8.0
CONV_W = 4
LSTM_CHUNK = 128
V7X_LANES = 128
V7X_SUBLANES = 8
VMEM_LIMIT = 56 * 1024 * 1024

F32 = jnp.float32
BF16 = jnp.bfloat16


def _params(sem):
    return pltpu.CompilerParams(dimension_semantics=sem, vmem_limit_bytes=VMEM_LIMIT)


def _mod_norm(x, scale, shift):
    ms = jnp.mean(x * x, axis=-1, keepdims=True)
    return (x * lax.rsqrt(ms + EPS)) * (1.0 + scale) + shift


BF16_SUBLANES = 16


def _cast_block(r, c, n_steps):
    col_blocks = 1
    while r * col_blocks < n_steps * BF16_SUBLANES:
        col_blocks *= 2
    assert (r * col_blocks) % n_steps == 0 and c % (col_blocks * V7X_LANES) == 0, (r, c, n_steps)
    return (r * col_blocks // n_steps, c // col_blocks), col_blocks


def _cast_specs(casts, n_steps, step_id):
    in_specs, out_specs, out_shapes = [], [], []
    for stacked, lead, cols in casts:
        r = stacked.shape[1]
        block, cb = _cast_block(r, cols, n_steps)
        src_index = lambda *ids, cb=cb, lead=lead: (lead, step_id(*ids) // cb, step_id(*ids) % cb)
        dst_index = lambda *ids, cb=cb: (step_id(*ids) // cb, step_id(*ids) % cb)
        in_specs.append(pl.BlockSpec((None,) + block, src_index))
        out_specs.append(pl.BlockSpec(block, dst_index))
        out_shapes.append(jax.ShapeDtypeStruct((r, cols), BF16))
    return in_specs, out_specs, out_shapes


def _run_casts(src_refs, dst_refs):
    for src, dst in zip(src_refs, dst_refs):
        dst[...] = src[...].astype(dst.dtype)


def _ada_kernel(c_ref, w_ref, b_ref, o_ref):
    c = c_ref[...]
    s = (c * jax.nn.sigmoid(c)).astype(BF16)
    o_ref[0] = jnp.dot(s, w_ref[0].astype(BF16), preferred_element_type=F32) + b_ref[0]


def _ada_mod(c, ada_w, ada_b, tn=1024):
    depth, two, d, n = ada_w.shape
    sites = depth * two
    b = c.shape[0]
    bp = -(-b // V7X_SUBLANES) * V7X_SUBLANES
    c_pad = jnp.pad(c, ((0, bp - b), (0, 0)))
    w = ada_w.reshape(sites, d, n)
    bias = ada_b.reshape(sites, 1, n)
    out = pl.pallas_call(
        _ada_kernel,
        grid=(sites, n // tn),
        in_specs=[
            pl.BlockSpec((bp, d), lambda s, j: (0, 0)),
            pl.BlockSpec((1, d, tn), lambda s, j: (s, 0, j)),
            pl.BlockSpec((1, 1, tn), lambda s, j: (s, 0, j)),
        ],
        out_specs=pl.BlockSpec((1, bp, tn), lambda s, j: (s, 0, j)),
        out_shape=jax.ShapeDtypeStruct((sites, bp, n), F32),
        compiler_params=_params(("parallel", "parallel")),
        name="ada_mod",
    )(c_pad, w, bias)
    return out[:, :b, :]


def _norm_matmul_kernel(*refs, has_gates, n_cast, cast_j):
    n_in = 4 + has_gates
    x_ref, sc_ref, sh_ref, w_ref = refs[:4]
    cast_src = refs[n_in:n_in + n_cast]
    outs = refs[n_in + n_cast:-1]
    o_ref, cast_dst, h_ref = outs[0], outs[1 + has_gates:], refs[-1]
    j = pl.program_id(2)

    def project():
        o_ref[0] = jnp.dot(h_ref[...], w_ref[...], preferred_element_type=F32).astype(o_ref.dtype)

    @pl.when(j == 0)
    def _():
        h_ref[...] = _mod_norm(x_ref[0], sc_ref[0], sh_ref[0]).astype(BF16)
        if has_gates:
            outs[1][0] = jnp.dot(h_ref[...], refs[4][...], preferred_element_type=F32)
        project()

    pl.when(j > 0)(project)

    if n_cast:
        pl.when(j < cast_j)(lambda: _run_casts(cast_src, cast_dst))


def _norm_matmul(x, scale, shift, w, w_gates=None, casts=(), tm=1024, tn=1024):
    b, s, d = x.shape
    n = w.shape[1]
    nj = n // tn
    grid = (b, s // tm, nj)
    cast_j = 1 << (nj.bit_length() - 1)
    step_id = lambda bi, i, j: (bi * (s // tm) + i) * cast_j + jnp.minimum(j, cast_j - 1)
    c_in, c_out, c_shapes = _cast_specs(casts, b * (s // tm) * cast_j, step_id)
    mod_spec = pl.BlockSpec((1, 1, d), lambda bi, i, j: (bi, 0, 0))
    in_specs = [pl.BlockSpec((1, tm, d), lambda bi, i, j: (bi, i, 0)), mod_spec, mod_spec,
                pl.BlockSpec((d, tn), lambda bi, i, j: (0, j))]
    out_specs = [pl.BlockSpec((1, tm, tn), lambda bi, i, j: (bi, i, j))]
    out_shapes = [jax.ShapeDtypeStruct((b, s, n), BF16)]
    args = [x, scale, shift, w]
    if w_gates is not None:
        ng = w_gates.shape[1]
        in_specs.append(pl.BlockSpec((d, ng), lambda bi, i, j: (0, 0)))
        out_specs.append(pl.BlockSpec((1, tm, ng), lambda bi, i, j: (bi, i, 0)))
        out_shapes.append(jax.ShapeDtypeStruct((b, s, ng), F32))
        args.append(w_gates)
    kernel = functools.partial(_norm_matmul_kernel, has_gates=w_gates is not None,
                               n_cast=len(casts), cast_j=cast_j)
    return pl.pallas_call(
        kernel, grid=grid,
        in_specs=in_specs + c_in, out_specs=out_specs + c_out, out_shape=out_shapes + c_shapes,
        scratch_shapes=[pltpu.VMEM((tm, d), BF16)],
        compiler_params=_params(("parallel", "parallel", "arbitrary")),
        name="norm_matmul_gates" if w_gates is not None else "norm_matmul",
    )(*args, *[stacked for stacked, _, _ in casts])


def _out_proj_kernel(y_ref, w_ref, x_ref, g_ref, o_ref):
    acc = jnp.dot(y_ref[0], w_ref[...], preferred_element_type=F32)
    o_ref[0] = x_ref[0] + g_ref[0] * acc


def _out_proj_residual(y, w, x, gate, tm=512, tn=2048):
    b, s, k = y.shape
    n = w.shape[1]
    return pl.pallas_call(
        _out_proj_kernel, grid=(b, s // tm, n // tn),
        in_specs=[
            pl.BlockSpec((1, tm, k), lambda bi, i, j: (bi, i, 0)),
            pl.BlockSpec((k, tn), lambda bi, i, j: (0, j)),
            pl.BlockSpec((1, tm, tn), lambda bi, i, j: (bi, i, j)),
            pl.BlockSpec((1, 1, tn), lambda bi, i, j: (bi, 0, j)),
        ],
        out_specs=pl.BlockSpec((1, tm, tn), lambda bi, i, j: (bi, i, j)),
        out_shape=jax.ShapeDtypeStruct((b, s, n), F32),
        compiler_params=_params(("parallel", "parallel", "arbitrary")),
        name="out_proj_residual",
    )(y, w, x, gate)


def _mlp_kernel(*refs, final_norm, n_cast):
    x_ref, sc_ref, sh_ref, g_ref, w1_ref, w2_ref, fg_ref = refs[:7]
    cast_src = refs[7:7 + n_cast]
    o_ref, cast_dst, h_ref = refs[7 + n_cast], refs[8 + n_cast:-1], refs[-1]
    f = pl.program_id(2)

    def ffn_slice():
        u = jnp.dot(h_ref[...], w1_ref[...], preferred_element_type=F32)
        u = jnp.square(jnp.maximum(u, 0.0)).astype(BF16)
        return g_ref[0] * jnp.dot(u, w2_ref[...], preferred_element_type=F32)

    @pl.when(f == 0)
    def _():
        h_ref[...] = _mod_norm(x_ref[0], sc_ref[0], sh_ref[0]).astype(BF16)
        o_ref[0] = x_ref[0] + ffn_slice()

    @pl.when(f > 0)
    def _():
        o_ref[0] += ffn_slice()

    _run_casts(cast_src, cast_dst)

    if final_norm:
        @pl.when(f == pl.num_programs(2) - 1)
        def _():
            y = o_ref[0]
            ms = jnp.mean(y * y, axis=-1, keepdims=True)
            o_ref[0] = (y * lax.rsqrt(ms + EPS)) * fg_ref[...]


def _mlp(x, scale, shift, gate, w1, w2, final_g, final_norm, casts=(), tm=1024, tf=512):
    b, s, d = x.shape
    dff = w1.shape[1]
    ni, nf = s // tm, dff // tf
    c_in, c_out, c_shapes = _cast_specs(casts, b * ni * nf, lambda bi, i, f: (bi * ni + i) * nf + f)
    mod_spec = pl.BlockSpec((1, 1, d), lambda bi, i, f: (bi, 0, 0))
    out = pl.pallas_call(
        functools.partial(_mlp_kernel, final_norm=final_norm, n_cast=len(casts)),
        grid=(b, ni, nf),
        in_specs=[
            pl.BlockSpec((1, tm, d), lambda bi, i, f: (bi, i, 0)),
            mod_spec, mod_spec, mod_spec,
            pl.BlockSpec((d, tf), lambda bi, i, f: (0, f)),
            pl.BlockSpec((tf, d), lambda bi, i, f: (f, 0)),
            pl.BlockSpec((1, d), lambda bi, i, f: (0, 0)),
        ] + c_in,
        out_specs=[pl.BlockSpec((1, tm, d), lambda bi, i, f: (bi, i, 0))] + c_out,
        out_shape=[jax.ShapeDtypeStruct((b, s, d), F32)] + c_shapes,
        scratch_shapes=[pltpu.VMEM((tm, d), BF16)],
        compiler_params=_params(("parallel", "parallel", "arbitrary")),
        name="mlp_final" if final_norm else "mlp",
    )(x, scale, shift, gate, w1, w2, final_g, *[stacked for stacked, _, _ in casts])
    return out[0], out[1:]


def _cumsum_rows(x, tril):
    x1 = x.astype(BF16)
    r1 = x - x1.astype(F32)
    x2 = r1.astype(BF16)
    x3 = (r1 - x2.astype(F32)).astype(BF16)
    d = lambda t: jnp.dot(tril, t, preferred_element_type=F32)
    return d(x1) + d(x2) + d(x3)


def _cummax_rows(x):
    n = x.shape[0]
    row = lax.broadcasted_iota(jnp.int32, x.shape, 0)
    sh = 1
    while sh < n:
        if sh < V7X_SUBLANES:
            prev = jnp.where(row >= sh, pltpu.roll(x, sh, 0), -jnp.inf)
        else:
            prev = jnp.concatenate([jnp.full((sh, x.shape[1]), -jnp.inf, x.dtype), x[:n - sh]], axis=0)
        x = jnp.maximum(x, prev)
        sh *= 2
    return x


def _log_sigmoid(z):
    return jnp.minimum(z, 0.0) - jnp.log1p(jnp.exp(-jnp.abs(z)))


def _mlstm_kernel(q_ref, k_ref, v_ref, o_ref, g_ref, bias_ref, ng_ref, y_ref, cn_ref, m_ref,
                  *, heads, dqk, dv):
    L, LN = LSTM_CHUNK, V7X_LANES

    @pl.when(pl.program_id(1) == 0)
    def _():
        cn_ref[...] = jnp.zeros_like(cn_ref)
        m_ref[...] = jnp.zeros_like(m_ref)

    row = lax.broadcasted_iota(jnp.int32, (L, L), 0)
    col = lax.broadcasted_iota(jnp.int32, (L, L), 1)
    causal = row >= col
    tril = jnp.where(causal, 1.0, 0.0).astype(BF16)
    eye = jnp.where(lax.broadcasted_iota(jnp.int32, (dqk, dqk), 0)
                    == lax.broadcasted_iota(jnp.int32, (dqk, dqk), 1), 1.0, 0.0).astype(BF16)

    log_in = g_ref[0, :, 0:LN] + bias_ref[:, 0:LN]
    b = _cumsum_rows(_log_sigmoid(g_ref[0, :, LN:2 * LN] + bias_ref[:, LN:2 * LN]), tril)
    r = log_in - b
    cm = _cummax_rows(r)
    m_row = m_ref[0:1, :]
    mx_last = jnp.maximum(m_row, cm[L - 1:L, :])
    decay_row = jnp.exp(m_row - mx_last)
    wk = jnp.exp(r - mx_last)
    m_ref[0:1, :] = b[L - 1:L, :] + mx_last
    r_rows = r.T
    wk_rows = wk.T

    floor_scale = float(dqk) ** 0.5
    ones_blk = jnp.ones((L, LN), BF16)
    nt = (((1,), (1,)), ((), ()))
    hr = range(heads)
    q = [q_ref[0, :, h * dqk:(h + 1) * dqk] for h in hr]
    k = [k_ref[0, :, h * dqk:(h + 1) * dqk] for h in hr]
    s = [lax.dot_general(q[h], k[h], nt, preferred_element_type=F32) for h in hr]
    k_t = [lax.dot_general(eye, k[h], nt, preferred_element_type=F32) for h in hr]

    lhs, rhs, vo, floor = [], [], [], []
    for h in hr:
        m_h = m_row[:, h:h + 1]
        mx = jnp.maximum(jnp.broadcast_to(cm[:, h:h + 1], (L, LN)), m_h)
        b_t = jnp.broadcast_to(b[:, h:h + 1], (L, LN))
        p = jnp.exp(jnp.where(causal, r_rows[h:h + 1, :] - mx, -jnp.inf)) * s[h]
        wq = (q[h].astype(F32) * jnp.exp(m_h - mx)).astype(BF16)
        lhs.append(jnp.concatenate([wq, p.astype(BF16)], axis=1))
        vo.append(jnp.concatenate([v_ref[0, :, h * dv:(h + 1) * dv], ones_blk], axis=1))
        rhs.append(jnp.concatenate([cn_ref[h].astype(BF16), vo[h]], axis=0))
        floor.append(jnp.exp(-(b_t + mx)) * floor_scale)

    out = [jnp.dot(lhs[h], rhs[h], preferred_element_type=F32) for h in hr]

    for h in hr:
        kw_t = (k_t[h] * wk_rows[h:h + 1, :]).astype(BF16)
        cn_ref[h] = (decay_row[:, h:h + 1] * cn_ref[h]
                     + jnp.dot(kw_t, vo[h], preferred_element_type=F32))

    for h in hr:
        inv = 1.0 / jnp.maximum(jnp.abs(out[h][:, dv:]), floor[h])
        hh = out[h][:, :dv] * jnp.concatenate([inv] * (dv // LN), axis=1)
        ms = jnp.mean(hh * hh, axis=1, keepdims=True)
        hn = (hh * lax.rsqrt(ms + EPS)) * ng_ref[:, h * dv:(h + 1) * dv]
        og = 0.5 * jnp.tanh(0.5 * o_ref[0, :, h * dv:(h + 1) * dv].astype(F32)) + 0.5
        y_ref[0, :, h * dv:(h + 1) * dv] = (hn * og).astype(y_ref.dtype)


def _mlstm(proj, gates, gate_bias, norm_g, heads, dqk, dv):
    b, s, _ = proj.shape
    L = LSTM_CHUNK
    qk_w, v_w = heads * dqk, heads * dv
    assert v_w == 2 * qk_w and dqk == L and heads <= V7X_LANES
    kernel = functools.partial(_mlstm_kernel, heads=heads, dqk=dqk, dv=dv)
    return pl.pallas_call(
        kernel, grid=(b, s // L),
        in_specs=[
            pl.BlockSpec((1, L, qk_w), lambda bi, c: (bi, c, 0)),
            pl.BlockSpec((1, L, qk_w), lambda bi, c: (bi, c, 1)),
            pl.BlockSpec((1, L, v_w), lambda bi, c: (bi, c, 1)),
            pl.BlockSpec((1, L, v_w), lambda bi, c: (bi, c, 2)),
            pl.BlockSpec((1, L, gates.shape[2]), lambda bi, c: (bi, c, 0)),
            pl.BlockSpec((1, gate_bias.shape[1]), lambda bi, c: (0, 0)),
            pl.BlockSpec((1, v_w), lambda bi, c: (0, 0)),
        ],
        out_specs=pl.BlockSpec((1, L, v_w), lambda bi, c: (bi, c, 0)),
        out_shape=jax.ShapeDtypeStruct((b, s, v_w), BF16),
        scratch_shapes=[pltpu.VMEM((heads, dqk, dv + V7X_LANES), F32),
                        pltpu.VMEM((V7X_SUBLANES, V7X_LANES), F32)],
        compiler_params=_params(("parallel", "arbitrary")),
        name="mlstm",
    )(proj, proj, proj, proj, gates, gate_bias, norm_g)


def _scan_permutation(tt):
    seg = tt // V7X_SUBLANES
    r = jnp.arange(tt)
    src = (r % V7X_SUBLANES) * seg + r // V7X_SUBLANES
    to_interleaved = (src[:, None] == jnp.arange(tt)[None, :]).astype(BF16)
    return to_interleaved, to_interleaved.T


def _sqrt_nonneg(x):
    return jnp.where(x > 0.0, x * lax.rsqrt(x), 0.0)


def _gelu_tanh(x):
    c = (2.0 / jnp.pi) ** 0.5
    hx = 0.5 * x
    return hx + hx * jnp.tanh(x * (c + (c * 0.044715) * (x * x)))


def _rglru_kernel(xb_ref, gb_ref, pm_ref, pmt_ref, cw_ref, cb_ref, wra_ref, bra_ref, wri_ref, bri_ref,
                  lam_ref, y_ref, a_ref, u_ref, tail_ref, carry_ref, *, tt, nblk, bw):
    S8 = V7X_SUBLANES
    seg = tt // S8
    H = CONV_W - 1
    d = a_ref.shape[1]

    @pl.when(pl.program_id(1) == 0)
    def _():
        tail_ref[...] = jnp.zeros_like(tail_ref)
        carry_ref[...] = jnp.zeros_like(carry_ref)

    pm = pm_ref[...]
    lam = lam_ref[...]
    half_nsp = (-0.5 * RG_C) * (jnp.maximum(-lam, 0.0) + jnp.log1p(jnp.exp(-jnp.abs(lam))))
    first_segment = lax.broadcasted_iota(jnp.int32, (S8, bw), 0) == 0

    for nb in range(nblk):
        sl = slice(nb * bw, (nb + 1) * bw)
        xp = jnp.dot(pm, xb_ref[0, :, sl], preferred_element_type=F32)
        hist = []
        for g in range(H):
            cur = xp[(seg - H + g) * S8:(seg - H + g + 1) * S8, :]
            prev = tail_ref[g * S8:(g + 1) * S8, sl]
            hist.append(jnp.where(first_segment, pltpu.roll(prev, 1, 0), pltpu.roll(cur, 1, 0)))
        tail_ref[:, sl] = xp[(seg - H) * S8:, :]
        xext = jnp.concatenate(hist + [xp], axis=0)
        xc = cb_ref[:, sl] + cw_ref[H:H + 1, sl] * xp
        for k in range(H):
            xc = xc + cw_ref[k:k + 1, sl] * xext[k * S8:k * S8 + tt, :]

        xc16 = xc.astype(BF16)
        t_r = jnp.tanh(jnp.dot(xc16, wra_ref[nb], preferred_element_type=F32) + bra_ref[:, sl])
        i = 0.5 * jnp.tanh(jnp.dot(xc16, wri_ref[nb], preferred_element_type=F32) + bri_ref[:, sl]) + 0.5
        log_a = t_r * half_nsp[:, sl] + half_nsp[:, sl]
        a = jnp.exp(log_a)
        one_m_a2 = jnp.tanh(log_a) * (-1.0 - a * a)
        a_ref[:, sl] = a
        u_ref[:, sl] = _sqrt_nonneg(one_m_a2) * (i * xc)

    def step(j, hp):
        h, p = hp
        rows = pl.ds(pl.multiple_of(j * S8, S8), S8)
        a = a_ref[rows, :]
        h = a * h + u_ref[rows, :]
        p = a * p
        u_ref[rows, :] = h
        a_ref[rows, :] = p
        return h, p

    h_end, p_end = lax.fori_loop(0, seg, step, (jnp.zeros((S8, d), F32), jnp.ones((S8, d), F32)),
                                 unroll=2)

    c = carry_ref[0:1, :]
    carry_rows = []
    for sgi in range(S8):
        carry_rows.append(c)
        c = h_end[sgi:sgi + 1, :] + p_end[sgi:sgi + 1, :] * c
    carry_ref[0:1, :] = c
    carry_in = jnp.concatenate(carry_rows, axis=0)

    pmt = pmt_ref[...]
    for nb in range(nblk):
        sl = slice(nb * bw, (nb + 1) * bw)
        cin = jnp.concatenate([carry_in[:, sl]] * seg, axis=0)
        hs = u_ref[:, sl] + a_ref[:, sl] * cin
        gp = jnp.dot(pm, gb_ref[0, :, sl], preferred_element_type=F32)
        yi = (hs * _gelu_tanh(gp)).astype(BF16)
        y_ref[0, :, sl] = jnp.dot(pmt, yi, preferred_element_type=F32).astype(y_ref.dtype)


def _rglru(proj, conv_w, conv_b, w_ra, b_ra, w_ri, b_ri, lam, tt=256):
    b, s, two_d = proj.shape
    d = two_d // 2
    nblk, bw, _ = w_ra.shape
    pm, pmt = _scan_permutation(tt)
    const = lambda shape: pl.BlockSpec(shape, lambda bi, i: (0,) * len(shape))
    kernel = functools.partial(_rglru_kernel, tt=tt, nblk=nblk, bw=bw)
    return pl.pallas_call(
        kernel, grid=(b, s // tt),
        in_specs=[
            pl.BlockSpec((1, tt, d), lambda bi, i: (bi, i, 0)),
            pl.BlockSpec((1, tt, d), lambda bi, i: (bi, i, 1)),
            const((tt, tt)), const((tt, tt)),
            const((CONV_W, d)), const((1, d)),
            const((nblk, bw, bw)), const((1, d)), const((nblk, bw, bw)), const((1, d)),
            const((1, d)),
        ],
        out_specs=pl.BlockSpec((1, tt, d), lambda bi, i: (bi, i, 0)),
        out_shape=jax.ShapeDtypeStruct((b, s, d), BF16),
        scratch_shapes=[pltpu.VMEM((tt, d), F32),
                        pltpu.VMEM((tt, d), F32),
                        pltpu.VMEM(((CONV_W - 1) * V7X_SUBLANES, d), F32),
                        pltpu.VMEM((V7X_SUBLANES, d), F32)],
        compiler_params=_params(("parallel", "arbitrary")),
        name="rglru",
    )(proj, proj, pm, pmt, conv_w, conv_b, w_ra, b_ra, w_ri, b_ri, lam)


def _gate_lanes(ig, fg):
    pad = [(0, 0)] * (ig.ndim - 1) + [(0, V7X_LANES - ig.shape[-1])]
    return jnp.concatenate([jnp.pad(ig, pad), jnp.pad(fg, pad)], axis=-1)


def kernel(x, c, ada_w, ada_b, a_w_in, a_b_gate, a_norm_g, a_w_out, b_w_in, b_conv_w, b_conv_b,
           b_w_ra, b_b_ra, b_w_ri, b_b_ri, b_lam, b_w_out, mlp_w1, mlp_w2, final_g):
    depth = ada_w.shape[0]
    d = x.shape[-1]
    heads = a_b_gate.shape[-1]
    v_w = a_norm_g.shape[-1]
    qk_w = (a_w_in.shape[-1] - 2 * v_w - 2 * heads) // 2
    dqk, dv = qk_w // heads, v_w // heads
    n_main = 2 * qk_w + 2 * v_w

    mod = _ada_mod(c, ada_w, ada_b)
    final_row = final_g.reshape(1, d)

    def site(layer, j):
        m = mod[layer * 2 + j]
        return (m[:, None, d:2 * d], m[:, None, 0:d], m[:, None, 2 * d:3 * d])

    def big_weights(layer):
        slot = layer // 2
        mixer = ((a_w_in, slot, n_main), (a_w_out, slot, d)) if layer % 2 == 0 else \
                ((b_w_in, slot, b_w_in.shape[-1]), (b_w_out, slot, d))
        return mixer + ((mlp_w1, layer, mlp_w1.shape[-1]), (mlp_w2, layer, d))

    first, lead, cols = big_weights(0)[0]
    w_in16 = first[lead][:, :cols].astype(BF16)
    for layer in range(depth):
        slot = layer // 2
        casts = big_weights(layer)[1:] if layer == 0 else ()
        scale, shift, gate = site(layer, 0)
        if layer % 2 == 0:
            w_g = a_w_in[slot][:, n_main:]
            w_gates = _gate_lanes(w_g[:, :heads], w_g[:, heads:]).astype(BF16)
            proj, gates, *converted = _norm_matmul(x, scale, shift, w_in16, w_gates, casts=casts)
            gate_bias = _gate_lanes(a_b_gate[slot, 0:1].astype(F32), a_b_gate[slot, 1:2].astype(F32))
            y = _mlstm(proj, gates, gate_bias, a_norm_g[slot].reshape(1, v_w).astype(F32),
                       heads, dqk, dv)
        else:
            proj, *converted = _norm_matmul(x, scale, shift, w_in16, casts=casts)
            drnn = b_lam.shape[-1]
            y = _rglru(proj, b_conv_w[slot], b_conv_b[slot].reshape(1, drnn),
                       (0.5 * b_w_ra[slot]).astype(BF16), 0.5 * b_b_ra[slot].reshape(1, drnn),
                       (0.5 * b_w_ri[slot]).astype(BF16), 0.5 * b_b_ri[slot].reshape(1, drnn),
                       b_lam[slot].reshape(1, drnn))
        if converted:
            w_out16, w1_16, w2_16 = converted
        x = _out_proj_residual(y, w_out16, x, gate)
        scale, shift, gate = site(layer, 1)
        last = layer == depth - 1
        x, converted = _mlp(x, scale, shift, gate, w1_16, w2_16, final_row, final_norm=last,
                            casts=() if last else big_weights(layer + 1))
        if converted:
            w_in16, w_out16, w1_16, w2_16 = converted
    return x
```

```python
import functools

import jax
import jax.numpy as jnp
from jax import lax
from jax.experimental import pallas as pl
from jax.experimental.pallas import tpu as pltpu

EPS = 1e-6
RG_C = 8.0
CONV_W = 4
LSTM_CHUNK = 128
V7X_LANES = 128
V7X_SUBLANES = 8
VMEM_LIMIT = 56 * 1024 * 1024

F32 = jnp.float32
BF16 = jnp.bfloat16


def _params(sem):
    return pltpu.CompilerParams(dimension_semantics=sem, vmem_limit_bytes=VMEM_LIMIT)


def _mod_norm(x, scale, shift):
    ms = jnp.mean(x * x, axis=-1, keepdims=True)
    return (x * lax.rsqrt(ms + EPS)) * (1.0 + scale) + shift


BF16_SUBLANES = 16


def _cast_block(r, c, n_steps):
    col_blocks = 1
    while r * col_blocks < n_steps * BF16_SUBLANES:
        col_blocks *= 2
    assert (r * col_blocks) % n_steps == 0 and c % (col_blocks * V7X_LANES) == 0, (r, c, n_steps)
    return (r * col_blocks // n_steps, c // col_blocks), col_blocks


def _cast_specs(casts, n_steps, step_id):
    in_specs, out_specs, out_shapes = [], [], []
    for stacked, lead, cols in casts:
        r = stacked.shape[1]
        block, cb = _cast_block(r, cols, n_steps)
        src_index = lambda *ids, cb=cb, lead=lead: (lead, step_id(*ids) // cb, step_id(*ids) % cb)
        dst_index = lambda *ids, cb=cb: (step_id(*ids) // cb, step_id(*ids) % cb)
        in_specs.append(pl.BlockSpec((None,) + block, src_index))
        out_specs.append(pl.BlockSpec(block, dst_index))
        out_shapes.append(jax.ShapeDtypeStruct((r, cols), BF16))
    return in_specs, out_specs, out_shapes


def _run_casts(src_refs, dst_refs):
    for src, dst in zip(src_refs, dst_refs):
        dst[...] = src[...].astype(dst.dtype)


def _ada_kernel(c_ref, w_ref, b_ref, o_ref):
    c = c_ref[...]
    s = (c * jax.nn.sigmoid(c)).astype(BF16)
    o_ref[0] = jnp.dot(s, w_ref[0].astype(BF16), preferred_element_type=F32) + b_ref[0]


def _ada_mod(c, ada_w, ada_b, tn=1024):
    depth, two, d, n = ada_w.shape
    sites = depth * two
    b = c.shape[0]
    bp = -(-b // V7X_SUBLANES) * V7X_SUBLANES
    c_pad = jnp.pad(c, ((0, bp - b), (0, 0)))
    w = ada_w.reshape(sites, d, n)
    bias = ada_b.reshape(sites, 1, n)
    out = pl.pallas_call(
        _ada_kernel,
        grid=(sites, n // tn),
        in_specs=[
            pl.BlockSpec((bp, d), lambda s, j: (0, 0)),
            pl.BlockSpec((1, d, tn), lambda s, j: (s, 0, j)),
            pl.BlockSpec((1, 1, tn), lambda s, j: (s, 0, j)),
        ],
        out_specs=pl.BlockSpec((1, bp, tn), lambda s, j: (s, 0, j)),
        out_shape=jax.ShapeDtypeStruct((sites, bp, n), F32),
        compiler_params=_params(("parallel", "parallel")),
        name="ada_mod",
    )(c_pad, w, bias)
    return out[:, :b, :]


def _norm_matmul_kernel(*refs, has_gates, n_cast, cast_j):
    n_in = 4 + has_gates
    x_ref, sc_ref, sh_ref, w_ref = refs[:4]
    cast_src = refs[n_in:n_in + n_cast]
    outs = refs[n_in + n_cast:-1]
    o_ref, cast_dst, h_ref = outs[0], outs[1 + has_gates:], refs[-1]
    j = pl.program_id(2)

    def project():
        o_ref[0] = jnp.dot(h_ref[...], w_ref[...], preferred_element_type=F32).astype(o_ref.dtype)

    @pl.when(j == 0)
    def _():
        h_ref[...] = _mod_norm(x_ref[0], sc_ref[0], sh_ref[0]).astype(BF16)
        if has_gates:
            outs[1][0] = jnp.dot(h_ref[...], refs[4][...], preferred_element_type=F32)
        project()

    pl.when(j > 0)(project)

    if n_cast:
        pl.when(j < cast_j)(lambda: _run_casts(cast_src, cast_dst))


def _norm_matmul(x, scale, shift, w, w_gates=None, casts=(), tm=1024, tn=1024):
    b, s, d = x.shape
    n = w.shape[1]
    nj = n // tn
    grid = (b, s // tm, nj)
    cast_j = 1 << (nj.bit_length() - 1)
    step_id = lambda bi, i, j: (bi * (s // tm) + i) * cast_j + jnp.minimum(j, cast_j - 1)
    c_in, c_out, c_shapes = _cast_specs(casts, b * (s // tm) * cast_j, step_id)
    mod_spec = pl.BlockSpec((1, 1, d), lambda bi, i, j: (bi, 0, 0))
    in_specs = [pl.BlockSpec((1, tm, d), lambda bi, i, j: (bi, i, 0)), mod_spec, mod_spec,
                pl.BlockSpec((d, tn), lambda bi, i, j: (0, j))]
    out_specs = [pl.BlockSpec((1, tm, tn), lambda bi, i, j: (bi, i, j))]
    out_shapes = [jax.ShapeDtypeStruct((b, s, n), BF16)]
    args = [x, scale, shift, w]
    if w_gates is not None:
        ng = w_gates.shape[1]
        in_specs.append(pl.BlockSpec((d, ng), lambda bi, i, j: (0, 0)))
        out_specs.append(pl.BlockSpec((1, tm, ng), lambda bi, i, j: (bi, i, 0)))
        out_shapes.append(jax.ShapeDtypeStruct((b, s, ng), F32))
        args.append(w_gates)
    kernel = functools.partial(_norm_matmul_kernel, has_gates=w_gates is not None,
                               n_cast=len(casts), cast_j=cast_j)
    return pl.pallas_call(
        kernel, grid=grid,
        in_specs=in_specs + c_in, out_specs=out_specs + c_out, out_shape=out_shapes + c_shapes,
        scratch_shapes=[pltpu.VMEM((tm, d), BF16)],
        compiler_params=_params(("parallel", "parallel", "arbitrary")),
        name="norm_matmul_gates" if w_gates is not None else "norm_matmul",
    )(*args, *[stacked for stacked, _, _ in casts])


def _out_proj_kernel(y_ref, w_ref, x_ref, g_ref, o_ref):
    acc = jnp.dot(y_ref[0], w_ref[...], preferred_element_type=F32)
    o_ref[0] = x_ref[0] + g_ref[0] * acc


def _out_proj_residual(y, w, x, gate, tm=512, tn=2048):
    b, s, k = y.shape
    n = w.shape[1]
    return pl.pallas_call(
        _out_proj_kernel, grid=(b, s // tm, n // tn),
        in_specs=[
            pl.BlockSpec((1, tm, k), lambda bi, i, j: (bi, i, 0)),
            pl.BlockSpec((k, tn), lambda bi, i, j: (0, j)),
            pl.BlockSpec((1, tm, tn), lambda bi, i, j: (bi, i, j)),
            pl.BlockSpec((1, 1, tn), lambda bi, i, j: (bi, 0, j)),
        ],
        out_specs=pl.BlockSpec((1, tm, tn), lambda bi, i, j: (bi, i, j)),
        out_shape=jax.ShapeDtypeStruct((b, s, n), F32),
        compiler_params=_params(("parallel", "parallel", "arbitrary")),
        name="out_proj_residual",
    )(y, w, x, gate)


def _mlp_kernel(*refs, final_norm, n_cast):
    x_ref, sc_ref, sh_ref, g_ref, w1_ref, w2_ref, fg_ref = refs[:7]
    cast_src = refs[7:7 + n_cast]
    o_ref, cast_dst, h_ref = refs[7 + n_cast], refs[8 + n_cast:-1], refs[-1]
    f = pl.program_id(2)

    def ffn_slice():
        u = jnp.dot(h_ref[...], w1_ref[...], preferred_element_type=F32)
        u = jnp.square(jnp.maximum(u, 0.0)).astype(BF16)
        return g_ref[0] * jnp.dot(u, w2_ref[...], preferred_element_type=F32)

    @pl.when(f == 0)
    def _():
        h_ref[...] = _mod_norm(x_ref[0], sc_ref[0], sh_ref[0]).astype(BF16)
        o_ref[0] = x_ref[0] + ffn_slice()

    @pl.when(f > 0)
    def _():
        o_ref[0] += ffn_slice()

    _run_casts(cast_src, cast_dst)

    if final_norm:
        @pl.when(f == pl.num_programs(2) - 1)
        def _():
            y = o_ref[0]
            ms = jnp.mean(y * y, axis=-1, keepdims=True)
            o_ref[0] = (y * lax.rsqrt(ms + EPS)) * fg_ref[...]


def _mlp(x, scale, shift, gate, w1, w2, final_g, final_norm, casts=(), tm=1024, tf=512):
    b, s, d = x.shape
    dff = w1.shape[1]
    ni, nf = s // tm, dff // tf
    c_in, c_out, c_shapes = _cast_specs(casts, b * ni * nf, lambda bi, i, f: (bi * ni + i) * nf + f)
    mod_spec = pl.BlockSpec((1, 1, d), lambda bi, i, f: (bi, 0, 0))
    out = pl.pallas_call(
        functools.partial(_mlp_kernel, final_norm=final_norm, n_cast=len(casts)),
        grid=(b, ni, nf),
        in_specs=[
            pl.BlockSpec((1, tm, d), lambda bi, i, f: (bi, i, 0)),
            mod_spec, mod_spec, mod_spec,
            pl.BlockSpec((d, tf), lambda bi, i, f: (0, f)),
            pl.BlockSpec((tf, d), lambda bi, i, f: (f, 0)),
            pl.BlockSpec((1, d), lambda bi, i, f: (0, 0)),
        ] + c_in,
        out_specs=[pl.BlockSpec((1, tm, d), lambda bi, i, f: (bi, i, 0))] + c_out,
        out_shape=[jax.ShapeDtypeStruct((b, s, d), F32)] + c_shapes,
        scratch_shapes=[pltpu.VMEM((tm, d), BF16)],
        compiler_params=_params(("parallel", "parallel", "arbitrary")),
        name="mlp_final" if final_norm else "mlp",
    )(x, scale, shift, gate, w1, w2, final_g, *[stacked for stacked, _, _ in casts])
    return out[0], out[1:]


def _cumsum_rows(x, tril):
    x1 = x.astype(BF16)
    r1 = x - x1.astype(F32)
    x2 = r1.astype(BF16)
    x3 = (r1 - x2.astype(F32)).astype(BF16)
    d = lambda t: jnp.dot(tril, t, preferred_element_type=F32)
    return d(x1) + d(x2) + d(x3)


def _cummax_rows(x):
    n = x.shape[0]
    row = lax.broadcasted_iota(jnp.int32, x.shape, 0)
    sh = 1
    while sh < n:
        if sh < V7X_SUBLANES:
            prev = jnp.where(row >= sh, pltpu.roll(x, sh, 0), -jnp.inf)
        else:
            prev = jnp.concatenate([jnp.full((sh, x.shape[1]), -jnp.inf, x.dtype), x[:n - sh]], axis=0)
        x = jnp.maximum(x, prev)
        sh *= 2
    return x


def _log_sigmoid(z):
    return jnp.minimum(z, 0.0) - jnp.log1p(jnp.exp(-jnp.abs(z)))


def _mlstm_kernel(q_ref, k_ref, v_ref, o_ref, g_ref, bias_ref, ng_ref, y_ref, cn_ref, m_ref,
                  *, heads, dqk, dv):
    L, LN = LSTM_CHUNK, V7X_LANES

    @pl.when(pl.program_id(1) == 0)
    def _():
        cn_ref[...] = jnp.zeros_like(cn_ref)
        m_ref[...] = jnp.zeros_like(m_ref)

    row = lax.broadcasted_iota(jnp.int32, (L, L), 0)
    col = lax.broadcasted_iota(jnp.int32, (L, L), 1)
    causal = row >= col
    tril = jnp.where(causal, 1.0, 0.0).astype(BF16)
    eye = jnp.where(lax.broadcasted_iota(jnp.int32, (dqk, dqk), 0)
                    == lax.broadcasted_iota(jnp.int32, (dqk, dqk), 1), 1.0, 0.0).astype(BF16)

    log_in = g_ref[0, :, 0:LN] + bias_ref[:, 0:LN]
    b = _cumsum_rows(_log_sigmoid(g_ref[0, :, LN:2 * LN] + bias_ref[:, LN:2 * LN]), tril)
    r = log_in - b
    cm = _cummax_rows(r)
    m_row = m_ref[0:1, :]
    mx_last = jnp.maximum(m_row, cm[L - 1:L, :])
    decay_row = jnp.exp(m_row - mx_last)
    wk = jnp.exp(r - mx_last)
    m_ref[0:1, :] = b[L - 1:L, :] + mx_last
    r_rows = r.T
    wk_rows = wk.T

    floor_scale = float(dqk) ** 0.5
    ones_blk = jnp.ones((L, LN), BF16)
    nt = (((1,), (1,)), ((), ()))
    hr = range(heads)
    q = [q_ref[0, :, h * dqk:(h + 1) * dqk] for h in hr]
    k = [k_ref[0, :, h * dqk:(h + 1) * dqk] for h in hr]
    s = [lax.dot_general(q[h], k[h], nt, preferred_element_type=F32) for h in hr]
    k_t = [lax.dot_general(eye, k[h], nt, preferred_element_type=F32) for h in hr]

    lhs, rhs, vo, floor = [], [], [], []
    for h in hr:
        m_h = m_row[:, h:h + 1]
        mx = jnp.maximum(jnp.broadcast_to(cm[:, h:h + 1], (L, LN)), m_h)
        b_t = jnp.broadcast_to(b[:, h:h + 1], (L, LN))
        p = jnp.exp(jnp.where(causal, r_rows[h:h + 1, :] - mx, -jnp.inf)) * s[h]
        wq = (q[h].astype(F32) * jnp.exp(m_h - mx)).astype(BF16)
        lhs.append(jnp.concatenate([wq, p.astype(BF16)], axis=1))
        vo.append(jnp.concatenate([v_ref[0, :, h * dv:(h + 1) * dv], ones_blk], axis=1))
        rhs.append(jnp.concatenate([cn_ref[h].astype(BF16), vo[h]], axis=0))
        floor.append(jnp.exp(-(b_t + mx)) * floor_scale)

    out = [jnp.dot(lhs[h], rhs[h], preferred_element_type=F32) for h in hr]

    for h in hr:
        kw_t = (k_t[h] * wk_rows[h:h + 1, :]).astype(BF16)
        cn_ref[h] = (decay_row[:, h:h + 1] * cn_ref[h]
                     + jnp.dot(kw_t, vo[h], preferred_element_type=F32))

    for h in hr:
        inv = 1.0 / jnp.maximum(jnp.abs(out[h][:, dv:]), floor[h])
        hh = out[h][:, :dv] * jnp.concatenate([inv] * (dv // LN), axis=1)
        ms = jnp.mean(hh * hh, axis=1, keepdims=True)
        hn = (hh * lax.rsqrt(ms + EPS)) * ng_ref[:, h * dv:(h + 1) * dv]
        og = 0.5 * jnp.tanh(0.5 * o_ref[0, :, h * dv:(h + 1) * dv].astype(F32)) + 0.5
        y_ref[0, :, h * dv:(h + 1) * dv] = (hn * og).astype(y_ref.dtype)


def _mlstm(proj, gates, gate_bias, norm_g, heads, dqk, dv):
    b, s, _ = proj.shape
    L = LSTM_CHUNK
    qk_w, v_w = heads * dqk, heads * dv
    assert v_w == 2 * qk_w and dqk == L and heads <= V7X_LANES
    kernel = functools.partial(_mlstm_kernel, heads=heads, dqk=dqk, dv=dv)
    return pl.pallas_call(
        kernel, grid=(b, s // L),
        in_specs=[
            pl.BlockSpec((1, L, qk_w), lambda bi, c: (bi, c, 0)),
            pl.BlockSpec((1, L, qk_w), lambda bi, c: (bi, c, 1)),
            pl.BlockSpec((1, L, v_w), lambda bi, c: (bi, c, 1)),
            pl.BlockSpec((1, L, v_w), lambda bi, c: (bi, c, 2)),
            pl.BlockSpec((1, L, gates.shape[2]), lambda bi, c: (bi, c, 0)),
            pl.BlockSpec((1, gate_bias.shape[1]), lambda bi, c: (0, 0)),
            pl.BlockSpec((1, v_w), lambda bi, c: (0, 0)),
        ],
        out_specs=pl.BlockSpec((1, L, v_w), lambda bi, c: (bi, c, 0)),
        out_shape=jax.ShapeDtypeStruct((b, s, v_w), BF16),
        scratch_shapes=[pltpu.VMEM((heads, dqk, dv + V7X_LANES), F32),
                        pltpu.VMEM((V7X_SUBLANES, V7X_LANES), F32)],
        compiler_params=_params(("parallel", "arbitrary")),
        name="mlstm",
    )(proj, proj, proj, proj, gates, gate_bias, norm_g)


def _scan_permutation(tt):
    seg = tt // V7X_SUBLANES
    r = jnp.arange(tt)
    src = (r % V7X_SUBLANES) * seg + r // V7X_SUBLANES
    to_interleaved = (src[:, None] == jnp.arange(tt)[None, :]).astype(BF16)
    return to_interleaved, to_interleaved.T


def _sqrt_nonneg(x):
    return jnp.where(x > 0.0, x * lax.rsqrt(x), 0.0)


def _gelu_tanh(x):
    c = (2.0 / jnp.pi) ** 0.5
    hx = 0.5 * x
    return hx + hx * jnp.tanh(x * (c + (c * 0.044715) * (x * x)))


def _rglru_layer_kernel(x_ref, sc_ref, sh_ref, g_ref, win_ref, wout_ref, pm_ref, pmt_ref, cw_ref, cb_ref,
                        wra_ref, bra_ref, wri_ref, bri_ref, lam_ref, o_ref,
                        hp_ref, y_ref, gp_ref, a_ref, u_ref, tail_ref, carry_ref, *, tt, nblk, bw):
    S8 = V7X_SUBLANES
    seg = tt // S8
    H = CONV_W - 1
    d = a_ref.shape[1]

    @pl.when(pl.program_id(1) == 0)
    def _():
        tail_ref[...] = jnp.zeros_like(tail_ref)
        carry_ref[...] = jnp.zeros_like(carry_ref)

    h = _mod_norm(x_ref[0], sc_ref[0], sh_ref[0]).astype(BF16)
    hp_ref[...] = jnp.dot(pm_ref[...], h, preferred_element_type=F32).astype(BF16)

    lam = lam_ref[...]
    half_nsp = (-0.5 * RG_C) * (jnp.maximum(-lam, 0.0) + jnp.log1p(jnp.exp(-jnp.abs(lam))))
    first_segment = lax.broadcasted_iota(jnp.int32, (S8, bw), 0) == 0

    def in_proj(first_col, group):
        cols = slice(first_col + group * GROUP * bw, first_col + (group + 1) * GROUP * bw)
        return jnp.dot(hp_ref[...], win_ref[:, cols], preferred_element_type=F32)

    GROUP = 2
    ahead = in_proj(0, 0)
    for nb in range(nblk):
        sl = slice(nb * bw, (nb + 1) * bw)
        if nb % GROUP == 0:
            group_out = ahead
            if nb + GROUP < nblk:
                ahead = in_proj(0, nb // GROUP + 1)
            gsl = slice(nb * bw, (nb + GROUP) * bw)
            gp_ref[:, gsl] = in_proj(d, nb // GROUP)
        xp = group_out[:, (nb % GROUP) * bw:(nb % GROUP + 1) * bw]
        hist = []
        for g in range(H):
            cur = xp[(seg - H + g) * S8:(seg - H + g + 1) * S8, :]
            prev = tail_ref[g * S8:(g + 1) * S8, sl]
            hist.append(jnp.where(first_segment, pltpu.roll(prev, 1, 0), pltpu.roll(cur, 1, 0)))
        tail_ref[:, sl] = xp[(seg - H) * S8:, :]
        xext = jnp.concatenate(hist + [xp], axis=0)
        xc = cb_ref[:, sl] + cw_ref[H:H + 1, sl] * xp
        for k in range(H):
            xc = xc + cw_ref[k:k + 1, sl] * xext[k * S8:k * S8 + tt, :]

        xc16 = xc.astype(BF16)
        t_r = jnp.tanh(jnp.dot(xc16, wra_ref[nb], preferred_element_type=F32) + bra_ref[:, sl])
        i = 0.5 * jnp.tanh(jnp.dot(xc16, wri_ref[nb], preferred_element_type=F32) + bri_ref[:, sl]) + 0.5
        log_a = t_r * half_nsp[:, sl] + half_nsp[:, sl]
        a = jnp.exp(log_a)
        one_m_a2 = jnp.tanh(log_a) * (-1.0 - a * a)
        a_ref[:, sl] = a
        u_ref[:, sl] = _sqrt_nonneg(one_m_a2) * (i * xc)

    def step(j, hp):
        h, p = hp
        rows = pl.ds(pl.multiple_of(j * S8, S8), S8)
        a = a_ref[rows, :]
        h = a * h + u_ref[rows, :]
        p = a * p
        u_ref[rows, :] = h
        a_ref[rows, :] = p
        return h, p

    h_end, p_end = lax.fori_loop(0, seg, step, (jnp.zeros((S8, d), F32), jnp.ones((S8, d), F32)),
                                 unroll=2)

    c = carry_ref[0:1, :]
    carry_rows = []
    for sgi in range(S8):
        carry_rows.append(c)
        c = h_end[sgi:sgi + 1, :] + p_end[sgi:sgi + 1, :] * c
    carry_ref[0:1, :] = c
    carry_in = jnp.concatenate(carry_rows, axis=0)

    pmt = pmt_ref[...]

    def out_proj(group):
        gsl = slice(group * GROUP * bw, (group + 1) * GROUP * bw)
        part = g_ref[0] * jnp.dot(y_ref[:, gsl], wout_ref[gsl, :], preferred_element_type=F32)
        if group == 0:
            o_ref[0] = x_ref[0] + part
        else:
            o_ref[0] += part

    for nb in range(nblk):
        sl = slice(nb * bw, (nb + 1) * bw)
        cin = jnp.concatenate([carry_in[:, sl]] * seg, axis=0)
        hs = u_ref[:, sl] + a_ref[:, sl] * cin
        yi = (hs * _gelu_tanh(gp_ref[:, sl])).astype(BF16)
        y_ref[:, sl] = jnp.dot(pmt, yi, preferred_element_type=F32).astype(BF16)
        if nb % GROUP == GROUP - 1 and nb >= 2 * GROUP - 1:
            out_proj(nb // GROUP - 1)
    out_proj(nblk // GROUP - 1)


def _rglru_layer(x, scale, shift, gate, w_in, w_out, conv_w, conv_b, w_ra, b_ra, w_ri, b_ri, lam, tt=256):
    b, s, dm = x.shape
    d = w_out.shape[0]
    nblk, bw, _ = w_ra.shape
    pm, pmt = _scan_permutation(tt)
    const = lambda shape: pl.BlockSpec(shape, lambda bi, i: (0,) * len(shape))
    resident = lambda shape: pl.BlockSpec(shape, lambda bi, i: (0,) * len(shape),
                                          pipeline_mode=pl.Buffered(1))
    mod_spec = pl.BlockSpec((1, 1, dm), lambda bi, i: (bi, 0, 0))
    kernel = functools.partial(_rglru_layer_kernel, tt=tt, nblk=nblk, bw=bw)
    return pl.pallas_call(
        kernel, grid=(b, s // tt),
        in_specs=[
            pl.BlockSpec((1, tt, dm), lambda bi, i: (bi, i, 0)),
            mod_spec, mod_spec, mod_spec,
            resident((dm, 2 * d)), resident((d, dm)),
            const((tt, tt)), const((tt, tt)),
            const((CONV_W, d)), const((1, d)),
            const((nblk, bw, bw)), const((1, d)), const((nblk, bw, bw)), const((1, d)),
            const((1, d)),
        ],
        out_specs=pl.BlockSpec((1, tt, dm), lambda bi, i: (bi, i, 0)),
        out_shape=jax.ShapeDtypeStruct((b, s, dm), F32),
        scratch_shapes=[pltpu.VMEM((tt, dm), BF16),
                        pltpu.VMEM((tt, d), BF16),
                        pltpu.VMEM((tt, d), F32),
                        pltpu.VMEM((tt, d), F32),
                        pltpu.VMEM((tt, d), F32),
                        pltpu.VMEM(((CONV_W - 1) * V7X_SUBLANES, d), F32),
                        pltpu.VMEM((V7X_SUBLANES, d), F32)],
        compiler_params=_params(("parallel", "arbitrary")),
        name="rglru_layer",
    )(x, scale, shift, gate, w_in, w_out, pm, pmt, conv_w, conv_b, w_ra, b_ra, w_ri, b_ri, lam)


def _gate_lanes(ig, fg):
    pad = [(0, 0)] * (ig.ndim - 1) + [(0, V7X_LANES - ig.shape[-1])]
    return jnp.concatenate([jnp.pad(ig, pad), jnp.pad(fg, pad)], axis=-1)


def kernel(x, c, ada_w, ada_b, a_w_in, a_b_gate, a_norm_g, a_w_out, b_w_in, b_conv_w, b_conv_b,
           b_w_ra, b_b_ra, b_w_ri, b_b_ri, b_lam, b_w_out, mlp_w1, mlp_w2, final_g):
    depth = ada_w.shape[0]
    d = x.shape[-1]
    heads = a_b_gate.shape[-1]
    v_w = a_norm_g.shape[-1]
    qk_w = (a_w_in.shape[-1] - 2 * v_w - 2 * heads) // 2
    dqk, dv = qk_w // heads, v_w // heads
    n_main = 2 * qk_w + 2 * v_w

    mod = _ada_mod(c, ada_w, ada_b)
    final_row = final_g.reshape(1, d)

    def site(layer, j):
        m = mod[layer * 2 + j]
        return (m[:, None, d:2 * d], m[:, None, 0:d], m[:, None, 2 * d:3 * d])

    def big_weights(layer):
        slot = layer // 2
        mixer = ((a_w_in, slot, n_main), (a_w_out, slot, d)) if layer % 2 == 0 else \
                ((b_w_in, slot, b_w_in.shape[-1]), (b_w_out, slot, d))
        return mixer + ((mlp_w1, layer, mlp_w1.shape[-1]), (mlp_w2, layer, d))

    first, lead, cols = big_weights(0)[0]
    w_in16 = first[lead][:, :cols].astype(BF16)
    for layer in range(depth):
        slot = layer // 2
        casts = big_weights(layer)[1:] if layer == 0 else ()
        scale, shift, gate = site(layer, 0)
        if layer % 2 == 0:
            w_g = a_w_in[slot][:, n_main:]
            w_gates = _gate_lanes(w_g[:, :heads], w_g[:, heads:]).astype(BF16)
            proj, gates, *converted = _norm_matmul(x, scale, shift, w_in16, w_gates, casts=casts)
            gate_bias = _gate_lanes(a_b_gate[slot, 0:1].astype(F32), a_b_gate[slot, 1:2].astype(F32))
            y = _mlstm(proj, gates, gate_bias, a_norm_g[slot].reshape(1, v_w).astype(F32),
                       heads, dqk, dv)
            if converted:
                w_out16, w1_16, w2_16 = converted
            x = _out_proj_residual(y, w_out16, x, gate)
        else:
            assert not casts
            drnn = b_lam.shape[-1]
            x = _rglru_layer(x, scale, shift, gate, w_in16, w_out16,
                             b_conv_w[slot], b_conv_b[slot].reshape(1, drnn),
                             (0.5 * b_w_ra[slot]).astype(BF16), 0.5 * b_b_ra[slot].reshape(1, drnn),
                             (0.5 * b_w_ri[slot]).astype(BF16), 0.5 * b_b_ri[slot].reshape(1, drnn),
                             b_lam[slot].reshape(1, drnn))
        scale, shift, gate = site(layer, 1)
        last = layer == depth - 1
        x, converted = _mlp(x, scale, shift, gate, w1_16, w2_16, final_row, final_norm=last,
                            casts=() if last else big_weights(layer + 1))
        if converted:
            w_in16, w_out16, w1_16, w2_16 = converted
    return x
```

```python
import functools

import jax
import jax.numpy as jnp
from jax import lax
from jax.experimental import pallas as pl
from jax.experimental.pallas import tpu as pltpu

EPS = 1e-6
RG_C = 8.0
CONV_W = 4
LSTM_CHUNK = 128
V7X_LANES = 128
V7X_SUBLANES = 8
VMEM_LIMIT = 56 * 1024 * 1024

F32 = jnp.float32
BF16 = jnp.bfloat16


def _params(sem):
    return pltpu.CompilerParams(dimension_semantics=sem, vmem_limit_bytes=VMEM_LIMIT)


def _mod_norm(x, scale, shift):
    ms = jnp.mean(x * x, axis=-1, keepdims=True)
    return (x * lax.rsqrt(ms + EPS)) * (1.0 + scale) + shift


BF16_SUBLANES = 16


def _cast_block(r, c, n_steps):
    col_blocks = 1
    while r * col_blocks < n_steps * BF16_SUBLANES:
        col_blocks *= 2
    assert (r * col_blocks) % n_steps == 0 and c % (col_blocks * V7X_LANES) == 0, (r, c, n_steps)
    return (r * col_blocks // n_steps, c // col_blocks), col_blocks


def _cast_specs(casts, n_steps, step_id):
    in_specs, out_specs, out_shapes = [], [], []
    for stacked, lead, cols in casts:
        r = stacked.shape[1]
        block, cb = _cast_block(r, cols, n_steps)
        src_index = lambda *ids, cb=cb, lead=lead: (lead, step_id(*ids) // cb, step_id(*ids) % cb)
        dst_index = lambda *ids, cb=cb: (step_id(*ids) // cb, step_id(*ids) % cb)
        in_specs.append(pl.BlockSpec((None,) + block, src_index))
        out_specs.append(pl.BlockSpec(block, dst_index))
        out_shapes.append(jax.ShapeDtypeStruct((r, cols), BF16))
    return in_specs, out_specs, out_shapes


def _run_casts(src_refs, dst_refs):
    for src, dst in zip(src_refs, dst_refs):
        dst[...] = src[...].astype(dst.dtype)


def _ada_kernel(c_ref, w_ref, b_ref, o_ref):
    c = c_ref[...]
    s = (c * jax.nn.sigmoid(c)).astype(BF16)
    o_ref[0] = jnp.dot(s, w_ref[0].astype(BF16), preferred_element_type=F32) + b_ref[0]


def _ada_mod(c, ada_w, ada_b, tn=1024):
    depth, two, d, n = ada_w.shape
    sites = depth * two
    b = c.shape[0]
    bp = -(-b // V7X_SUBLANES) * V7X_SUBLANES
    c_pad = jnp.pad(c, ((0, bp - b), (0, 0)))
    w = ada_w.reshape(sites, d, n)
    bias = ada_b.reshape(sites, 1, n)
    out = pl.pallas_call(
        _ada_kernel,
        grid=(sites, n // tn),
        in_specs=[
            pl.BlockSpec((bp, d), lambda s, j: (0, 0)),
            pl.BlockSpec((1, d, tn), lambda s, j: (s, 0, j)),
            pl.BlockSpec((1, 1, tn), lambda s, j: (s, 0, j)),
        ],
        out_specs=pl.BlockSpec((1, bp, tn), lambda s, j: (s, 0, j)),
        out_shape=jax.ShapeDtypeStruct((sites, bp, n), F32),
        compiler_params=_params(("parallel", "parallel")),
        name="ada_mod",
    )(c_pad, w, bias)
    return out[:, :b, :]


def _norm_matmul_kernel(*refs, has_gates, n_cast, cast_j):
    n_in = 4 + has_gates
    x_ref, sc_ref, sh_ref, w_ref = refs[:4]
    cast_src = refs[n_in:n_in + n_cast]
    outs = refs[n_in + n_cast:-1]
    o_ref, cast_dst, h_ref = outs[0], outs[1 + has_gates:], refs[-1]
    j = pl.program_id(2)

    def project():
        o_ref[0] = jnp.dot(h_ref[...], w_ref[...], preferred_element_type=F32).astype(o_ref.dtype)

    @pl.when(j == 0)
    def _():
        h_ref[...] = _mod_norm(x_ref[0], sc_ref[0], sh_ref[0]).astype(BF16)
        if has_gates:
            outs[1][0] = jnp.dot(h_ref[...], refs[4][...], preferred_element_type=F32)
        project()

    pl.when(j > 0)(project)

    if n_cast:
        pl.when(j < cast_j)(lambda: _run_casts(cast_src, cast_dst))


def _norm_matmul(x, scale, shift, w, w_gates=None, casts=(), tm=1024, tn=1024):
    b, s, d = x.shape
    n = w.shape[1]
    nj = n // tn
    grid = (b, s // tm, nj)
    cast_j = 1 << (nj.bit_length() - 1)
    step_id = lambda bi, i, j: (bi * (s // tm) + i) * cast_j + jnp.minimum(j, cast_j - 1)
    c_in, c_out, c_shapes = _cast_specs(casts, b * (s // tm) * cast_j, step_id)
    mod_spec = pl.BlockSpec((1, 1, d), lambda bi, i, j: (bi, 0, 0))
    in_specs = [pl.BlockSpec((1, tm, d), lambda bi, i, j: (bi, i, 0)), mod_spec, mod_spec,
                pl.BlockSpec((d, tn), lambda bi, i, j: (0, j))]
    out_specs = [pl.BlockSpec((1, tm, tn), lambda bi, i, j: (bi, i, j))]
    out_shapes = [jax.ShapeDtypeStruct((b, s, n), BF16)]
    args = [x, scale, shift, w]
    if w_gates is not None:
        ng = w_gates.shape[1]
        in_specs.append(pl.BlockSpec((d, ng), lambda bi, i, j: (0, 0)))
        out_specs.append(pl.BlockSpec((1, tm, ng), lambda bi, i, j: (bi, i, 0)))
        out_shapes.append(jax.ShapeDtypeStruct((b, s, ng), F32))
        args.append(w_gates)
    kernel = functools.partial(_norm_matmul_kernel, has_gates=w_gates is not None,
                               n_cast=len(casts), cast_j=cast_j)
    return pl.pallas_call(
        kernel, grid=grid,
        in_specs=in_specs + c_in, out_specs=out_specs + c_out, out_shape=out_shapes + c_shapes,
        scratch_shapes=[pltpu.VMEM((tm, d), BF16)],
        compiler_params=_params(("parallel", "parallel", "arbitrary")),
        name="norm_matmul_gates" if w_gates is not None else "norm_matmul",
    )(*args, *[stacked for stacked, _, _ in casts])


def _out_proj_kernel(y_ref, w_ref, x_ref, g_ref, o_ref):
    acc = jnp.dot(y_ref[0], w_ref[...], preferred_element_type=F32)
    o_ref[0] = x_ref[0] + g_ref[0] * acc


def _out_proj_residual(y, w, x, gate, tm=512, tn=2048):
    b, s, k = y.shape
    n = w.shape[1]
    return pl.pallas_call(
        _out_proj_kernel, grid=(b, s // tm, n // tn),
        in_specs=[
            pl.BlockSpec((1, tm, k), lambda bi, i, j: (bi, i, 0)),
            pl.BlockSpec((k, tn), lambda bi, i, j: (0, j)),
            pl.BlockSpec((1, tm, tn), lambda bi, i, j: (bi, i, j)),
            pl.BlockSpec((1, 1, tn), lambda bi, i, j: (bi, 0, j)),
        ],
        out_specs=pl.BlockSpec((1, tm, tn), lambda bi, i, j: (bi, i, j)),
        out_shape=jax.ShapeDtypeStruct((b, s, n), F32),
        compiler_params=_params(("parallel", "parallel", "arbitrary")),
        name="out_proj_residual",
    )(y, w, x, gate)


def _mlp_kernel(*refs, final_norm, n_cast):
    x_ref, sc_ref, sh_ref, g_ref, w1_ref, w2_ref, fg_ref = refs[:7]
    cast_src = refs[7:7 + n_cast]
    o_ref, cast_dst, h_ref = refs[7 + n_cast], refs[8 + n_cast:-1], refs[-1]
    f = pl.program_id(2)

    def ffn_slice():
        u = jnp.dot(h_ref[...], w1_ref[...], preferred_element_type=F32)
        u = jnp.square(jnp.maximum(u, 0.0)).astype(BF16)
        return g_ref[0] * jnp.dot(u, w2_ref[...], preferred_element_type=F32)

    @pl.when(f == 0)
    def _():
        h_ref[...] = _mod_norm(x_ref[0], sc_ref[0], sh_ref[0]).astype(BF16)
        o_ref[0] = x_ref[0] + ffn_slice()

    @pl.when(f > 0)
    def _():
        o_ref[0] += ffn_slice()

    _run_casts(cast_src, cast_dst)

    if final_norm:
        @pl.when(f == pl.num_programs(2) - 1)
        def _():
            y = o_ref[0]
            ms = jnp.mean(y * y, axis=-1, keepdims=True)
            o_ref[0] = (y * lax.rsqrt(ms + EPS)) * fg_ref[...]


def _mlp(x, scale, shift, gate, w1, w2, final_g, final_norm, casts=(), tm=1024, tf=512):
    b, s, d = x.shape
    dff = w1.shape[1]
    ni, nf = s // tm, dff // tf
    c_in, c_out, c_shapes = _cast_specs(casts, b * ni * nf, lambda bi, i, f: (bi * ni + i) * nf + f)
    mod_spec = pl.BlockSpec((1, 1, d), lambda bi, i, f: (bi, 0, 0))
    out = pl.pallas_call(
        functools.partial(_mlp_kernel, final_norm=final_norm, n_cast=len(casts)),
        grid=(b, ni, nf),
        in_specs=[
            pl.BlockSpec((1, tm, d), lambda bi, i, f: (bi, i, 0)),
            mod_spec, mod_spec, mod_spec,
            pl.BlockSpec((d, tf), lambda bi, i, f: (0, f)),
            pl.BlockSpec((tf, d), lambda bi, i, f: (f, 0)),
            pl.BlockSpec((1, d), lambda bi, i, f: (0, 0)),
        ] + c_in,
        out_specs=[pl.BlockSpec((1, tm, d), lambda bi, i, f: (bi, i, 0))] + c_out,
        out_shape=[jax.ShapeDtypeStruct((b, s, d), F32)] + c_shapes,
        scratch_shapes=[pltpu.VMEM((tm, d), BF16)],
        compiler_params=_params(("parallel", "parallel", "arbitrary")),
        name="mlp_final" if final_norm else "mlp",
    )(x, scale, shift, gate, w1, w2, final_g, *[stacked for stacked, _, _ in casts])
    return out[0], out[1:]


def _cumsum_rows(x, tril):
    x1 = x.astype(BF16)
    r1 = x - x1.astype(F32)
    x2 = r1.astype(BF16)
    x3 = (r1 - x2.astype(F32)).astype(BF16)
    d = lambda t: jnp.dot(tril, t, preferred_element_type=F32)
    return d(x1) + d(x2) + d(x3)


def _cummax_rows(x):
    n = x.shape[0]
    row = lax.broadcasted_iota(jnp.int32, x.shape, 0)
    sh = 1
    while sh < n:
        if sh < V7X_SUBLANES:
            prev = jnp.where(row >= sh, pltpu.roll(x, sh, 0), -jnp.inf)
        else:
            prev = jnp.concatenate([jnp.full((sh, x.shape[1]), -jnp.inf, x.dtype), x[:n - sh]], axis=0)
        x = jnp.maximum(x, prev)
        sh *= 2
    return x


def _log_sigmoid(z):
    return jnp.minimum(z, 0.0) - jnp.log1p(jnp.exp(-jnp.abs(z)))


def _mlstm_kernel(q_ref, k_ref, v_ref, o_ref, g_ref, bias_ref, ng_ref, y_ref, cn_ref, m_ref,
                  *, heads, dqk, dv):
    L, LN = LSTM_CHUNK, V7X_LANES

    @pl.when(pl.program_id(1) == 0)
    def _():
        cn_ref[...] = jnp.zeros_like(cn_ref)
        m_ref[...] = jnp.zeros_like(m_ref)

    row = lax.broadcasted_iota(jnp.int32, (L, L), 0)
    col = lax.broadcasted_iota(jnp.int32, (L, L), 1)
    causal = row >= col
    tril = jnp.where(causal, 1.0, 0.0).astype(BF16)
    eye = jnp.where(lax.broadcasted_iota(jnp.int32, (dqk, dqk), 0)
                    == lax.broadcasted_iota(jnp.int32, (dqk, dqk), 1), 1.0, 0.0).astype(BF16)

    log_in = g_ref[0, :, 0:LN] + bias_ref[:, 0:LN]
    b = _cumsum_rows(_log_sigmoid(g_ref[0, :, LN:2 * LN] + bias_ref[:, LN:2 * LN]), tril)
    r = log_in - b
    cm = _cummax_rows(r)
    m_row = m_ref[0:1, :]
    mx_last = jnp.maximum(m_row, cm[L - 1:L, :])
    decay_row = jnp.exp(m_row - mx_last)
    wk = jnp.exp(r - mx_last)
    m_ref[0:1, :] = b[L - 1:L, :] + mx_last
    r_rows = r.T
    wk_rows = wk.T

    floor_scale = float(dqk) ** 0.5
    ones_blk = jnp.ones((L, LN), BF16)
    nt = (((1,), (1,)), ((), ()))
    hr = range(heads)
    q = [q_ref[0, :, h * dqk:(h + 1) * dqk] for h in hr]
    k = [k_ref[0, :, h * dqk:(h + 1) * dqk] for h in hr]
    s = [lax.dot_general(q[h], k[h], nt, preferred_element_type=F32) for h in hr]
    k_t = [lax.dot_general(eye, k[h], nt, preferred_element_type=F32) for h in hr]

    lhs, rhs, vo, floor = [], [], [], []
    for h in hr:
        m_h = m_row[:, h:h + 1]
        mx = jnp.maximum(jnp.broadcast_to(cm[:, h:h + 1], (L, LN)), m_h)
        b_t = jnp.broadcast_to(b[:, h:h + 1], (L, LN))
        p = jnp.exp(jnp.where(causal, r_rows[h:h + 1, :] - mx, -jnp.inf)) * s[h]
        wq = (q[h].astype(F32) * jnp.exp(m_h - mx)).astype(BF16)
        lhs.append(jnp.concatenate([wq, p.astype(BF16)], axis=1))
        vo.append(jnp.concatenate([v_ref[0, :, h * dv:(h + 1) * dv], ones_blk], axis=1))
        rhs.append(jnp.concatenate([cn_ref[h].astype(BF16), vo[h]], axis=0))
        floor.append(jnp.exp(-(b_t + mx)) * floor_scale)

    out = [jnp.dot(lhs[h], rhs[h], preferred_element_type=F32) for h in hr]

    for h in hr:
        kw_t = (k_t[h] * wk_rows[h:h + 1, :]).astype(BF16)
        cn_ref[h] = (decay_row[:, h:h + 1] * cn_ref[h]
                     + jnp.dot(kw_t, vo[h], preferred_element_type=F32))

    for h in hr:
        inv = 1.0 / jnp.maximum(jnp.abs(out[h][:, dv:]), floor[h])
        hh = out[h][:, :dv] * jnp.concatenate([inv] * (dv // LN), axis=1)
        ms = jnp.mean(hh * hh, axis=1, keepdims=True)
        hn = (hh * lax.rsqrt(ms + EPS)) * ng_ref[:, h * dv:(h + 1) * dv]
        og = 0.5 * jnp.tanh(0.5 * o_ref[0, :, h * dv:(h + 1) * dv].astype(F32)) + 0.5
        y_ref[0, :, h * dv:(h + 1) * dv] = (hn * og).astype(y_ref.dtype)


def _mlstm(proj, gates, gate_bias, norm_g, heads, dqk, dv):
    b, s, _ = proj.shape
    L = LSTM_CHUNK
    qk_w, v_w = heads * dqk, heads * dv
    assert v_w == 2 * qk_w and dqk == L and heads <= V7X_LANES
    kernel = functools.partial(_mlstm_kernel, heads=heads, dqk=dqk, dv=dv)
    return pl.pallas_call(
        kernel, grid=(b, s // L),
        in_specs=[
            pl.BlockSpec((1, L, qk_w), lambda bi, c: (bi, c, 0)),
            pl.BlockSpec((1, L, qk_w), lambda bi, c: (bi, c, 1)),
            pl.BlockSpec((1, L, v_w), lambda bi, c: (bi, c, 1)),
            pl.BlockSpec((1, L, v_w), lambda bi, c: (bi, c, 2)),
            pl.BlockSpec((1, L, gates.shape[2]), lambda bi, c: (bi, c, 0)),
            pl.BlockSpec((1, gate_bias.shape[1]), lambda bi, c: (0, 0)),
            pl.BlockSpec((1, v_w), lambda bi, c: (0, 0)),
        ],
        out_specs=pl.BlockSpec((1, L, v_w), lambda bi, c: (bi, c, 0)),
        out_shape=jax.ShapeDtypeStruct((b, s, v_w), BF16),
        scratch_shapes=[pltpu.VMEM((heads, dqk, dv + V7X_LANES), F32),
                        pltpu.VMEM((V7X_SUBLANES, V7X_LANES), F32)],
        compiler_params=_params(("parallel", "arbitrary")),
        name="mlstm",
    )(proj, proj, proj, proj, gates, gate_bias, norm_g)


def _scan_permutation(tt):
    seg = tt // V7X_SUBLANES
    r = jnp.arange(tt)
    src = (r % V7X_SUBLANES) * seg + r // V7X_SUBLANES
    to_interleaved = (src[:, None] == jnp.arange(tt)[None, :]).astype(BF16)
    return to_interleaved, to_interleaved.T


def _sqrt_nonneg(x):
    return jnp.where(x > 0.0, x * lax.rsqrt(x), 0.0)


def _gelu_tanh(x):
    c = (2.0 / jnp.pi) ** 0.5
    hx = 0.5 * x
    return hx + hx * jnp.tanh(x * (c + (c * 0.044715) * (x * x)))


def _rglru_layer_kernel(x_ref, sc_ref, sh_ref, g_ref, win_ref, wout_ref, pm_ref, pmt_ref, cw_ref, cb_ref,
                        wra_ref, bra_ref, wri_ref, bri_ref, lam_ref, o_ref,
                        hp_ref, y_ref, gp_ref, a_ref, u_ref, tail_ref, carry_ref, *, tt, nblk, bw):
    S8 = V7X_SUBLANES
    seg = tt // S8
    H = CONV_W - 1
    d = a_ref.shape[1]

    @pl.when(pl.program_id(1) == 0)
    def _():
        tail_ref[...] = jnp.zeros_like(tail_ref)
        carry_ref[...] = jnp.zeros_like(carry_ref)

    h = _mod_norm(x_ref[0], sc_ref[0], sh_ref[0]).astype(BF16)
    hp_ref[...] = jnp.dot(pm_ref[...], h, preferred_element_type=F32).astype(BF16)

    lam = lam_ref[...]
    half_nsp = (-0.5 * RG_C) * (jnp.maximum(-lam, 0.0) + jnp.log1p(jnp.exp(-jnp.abs(lam))))
    first_segment = lax.broadcasted_iota(jnp.int32, (S8, bw), 0) == 0

    def in_proj(first_col, group):
        cols = slice(first_col + group * GROUP * bw, first_col + (group + 1) * GROUP * bw)
        return jnp.dot(hp_ref[...], win_ref[:, cols], preferred_element_type=F32)

    def project_gate(group):
        gp_ref[:, group * GROUP * bw:(group + 1) * GROUP * bw] = in_proj(d, group)

    GROUP = 2
    early_gate_groups = nblk // GROUP // 2
    ahead = in_proj(0, 0)
    for nb in range(nblk):
        sl = slice(nb * bw, (nb + 1) * bw)
        if nb % GROUP == 0:
            group_out = ahead
            if nb + GROUP < nblk:
                ahead = in_proj(0, nb // GROUP + 1)
            if nb // GROUP < early_gate_groups:
                project_gate(nb // GROUP)
        xp = group_out[:, (nb % GROUP) * bw:(nb % GROUP + 1) * bw]
        hist = []
        for g in range(H):
            cur = xp[(seg - H + g) * S8:(seg - H + g + 1) * S8, :]
            prev = tail_ref[g * S8:(g + 1) * S8, sl]
            hist.append(jnp.where(first_segment, pltpu.roll(prev, 1, 0), pltpu.roll(cur, 1, 0)))
        tail_ref[:, sl] = xp[(seg - H) * S8:, :]
        xext = jnp.concatenate(hist + [xp], axis=0)
        xc = cb_ref[:, sl] + cw_ref[H:H + 1, sl] * xp
        for k in range(H):
            xc = xc + cw_ref[k:k + 1, sl] * xext[k * S8:k * S8 + tt, :]

        xc16 = xc.astype(BF16)
        t_r = jnp.tanh(jnp.dot(xc16, wra_ref[nb], preferred_element_type=F32) + bra_ref[:, sl])
        i = 0.5 * jnp.tanh(jnp.dot(xc16, wri_ref[nb], preferred_element_type=F32) + bri_ref[:, sl]) + 0.5
        log_a = t_r * half_nsp[:, sl] + half_nsp[:, sl]
        a = jnp.exp(log_a)
        one_m_a2 = jnp.tanh(log_a) * (-1.0 - a * a)
        a_ref[:, sl] = a
        u_ref[:, sl] = _sqrt_nonneg(one_m_a2) * (i * xc)

    for group in range(early_gate_groups, nblk // GROUP):
        project_gate(group)
    h_end, p_end = jnp.zeros((S8, d), F32), jnp.ones((S8, d), F32)
    for j in range(seg):
        rows = slice(j * S8, (j + 1) * S8)
        a = a_ref[rows, :]
        h_end = a * h_end + u_ref[rows, :]
        p_end = a * p_end
        u_ref[rows, :] = h_end
        a_ref[rows, :] = p_end

    c = carry_ref[0:1, :]
    carry_rows = []
    for sgi in range(S8):
        carry_rows.append(c)
        c = h_end[sgi:sgi + 1, :] + p_end[sgi:sgi + 1, :] * c
    carry_ref[0:1, :] = c
    carry_in = jnp.concatenate(carry_rows, axis=0)

    pmt = pmt_ref[...]

    def out_proj(group):
        gsl = slice(group * GROUP * bw, (group + 1) * GROUP * bw)
        part = g_ref[0] * jnp.dot(y_ref[:, gsl], wout_ref[gsl, :], preferred_element_type=F32)
        if group == 0:
            o_ref[0] = x_ref[0] + part
        else:
            o_ref[0] += part

    for nb in range(nblk):
        sl = slice(nb * bw, (nb + 1) * bw)
        cin = jnp.concatenate([carry_in[:, sl]] * seg, axis=0)
        hs = u_ref[:, sl] + a_ref[:, sl] * cin
        yi = (hs * _gelu_tanh(gp_ref[:, sl])).astype(BF16)
        y_ref[:, sl] = jnp.dot(pmt, yi, preferred_element_type=F32).astype(BF16)
        if nb % GROUP == GROUP - 1 and nb >= 2 * GROUP - 1:
            out_proj(nb // GROUP - 1)
    out_proj(nblk // GROUP - 1)


def _rglru_layer(x, scale, shift, gate, w_in, w_out, conv_w, conv_b, w_ra, b_ra, w_ri, b_ri, lam, tt=256):
    b, s, dm = x.shape
    d = w_out.shape[0]
    nblk, bw, _ = w_ra.shape
    pm, pmt = _scan_permutation(tt)
    const = lambda shape: pl.BlockSpec(shape, lambda bi, i: (0,) * len(shape))
    resident = lambda shape: pl.BlockSpec(shape, lambda bi, i: (0,) * len(shape),
                                          pipeline_mode=pl.Buffered(1))
    mod_spec = pl.BlockSpec((1, 1, dm), lambda bi, i: (bi, 0, 0))
    kernel = functools.partial(_rglru_layer_kernel, tt=tt, nblk=nblk, bw=bw)
    return pl.pallas_call(
        kernel, grid=(b, s // tt),
        in_specs=[
            pl.BlockSpec((1, tt, dm), lambda bi, i: (bi, i, 0)),
            mod_spec, mod_spec, mod_spec,
            resident((dm, 2 * d)), resident((d, dm)),
            const((tt, tt)), const((tt, tt)),
            const((CONV_W, d)), const((1, d)),
            const((nblk, bw, bw)), const((1, d)), const((nblk, bw, bw)), const((1, d)),
            const((1, d)),
        ],
        out_specs=pl.BlockSpec((1, tt, dm), lambda bi, i: (bi, i, 0)),
        out_shape=jax.ShapeDtypeStruct((b, s, dm), F32),
        scratch_shapes=[pltpu.VMEM((tt, dm), BF16),
                        pltpu.VMEM((tt, d), BF16),
                        pltpu.VMEM((tt, d), F32),
                        pltpu.VMEM((tt, d), F32),
                        pltpu.VMEM((tt, d), F32),
                        pltpu.VMEM(((CONV_W - 1) * V7X_SUBLANES, d), F32),
                        pltpu.VMEM((V7X_SUBLANES, d), F32)],
        compiler_params=_params(("parallel", "arbitrary")),
        name="rglru_layer",
    )(x, scale, shift, gate, w_in, w_out, pm, pmt, conv_w, conv_b, w_ra, b_ra, w_ri, b_ri, lam)


def _gate_lanes(ig, fg):
    pad = [(0, 0)] * (ig.ndim - 1) + [(0, V7X_LANES - ig.shape[-1])]
    return jnp.concatenate([jnp.pad(ig, pad), jnp.pad(fg, pad)], axis=-1)


def kernel(x, c, ada_w, ada_b, a_w_in, a_b_gate, a_norm_g, a_w_out, b_w_in, b_conv_w, b_conv_b,
           b_w_ra, b_b_ra, b_w_ri, b_b_ri, b_lam, b_w_out, mlp_w1, mlp_w2, final_g):
    depth = ada_w.shape[0]
    d = x.shape[-1]
    heads = a_b_gate.shape[-1]
    v_w = a_norm_g.shape[-1]
    qk_w = (a_w_in.shape[-1] - 2 * v_w - 2 * heads) // 2
    dqk, dv = qk_w // heads, v_w // heads
    n_main = 2 * qk_w + 2 * v_w

    mod = _ada_mod(c, ada_w, ada_b)
    final_row = final_g.reshape(1, d)

    def site(layer, j):
        m = mod[layer * 2 + j]
        return (m[:, None, d:2 * d], m[:, None, 0:d], m[:, None, 2 * d:3 * d])

    def big_weights(layer):
        slot = layer // 2
        mixer = ((a_w_in, slot, n_main), (a_w_out, slot, d)) if layer % 2 == 0 else \
                ((b_w_in, slot, b_w_in.shape[-1]), (b_w_out, slot, d))
        return mixer + ((mlp_w1, layer, mlp_w1.shape[-1]), (mlp_w2, layer, d))

    first, lead, cols = big_weights(0)[0]
    w_in16 = first[lead][:, :cols].astype(BF16)
    for layer in range(depth):
        slot = layer // 2
        casts = big_weights(layer)[1:] if layer == 0 else ()
        scale, shift, gate = site(layer, 0)
        if layer % 2 == 0:
            w_g = a_w_in[slot][:, n_main:]
            w_gates = _gate_lanes(w_g[:, :heads], w_g[:, heads:]).astype(BF16)
            proj, gates, *converted = _norm_matmul(x, scale, shift, w_in16, w_gates, casts=casts)
            gate_bias = _gate_lanes(a_b_gate[slot, 0:1].astype(F32), a_b_gate[slot, 1:2].astype(F32))
            y = _mlstm(proj, gates, gate_bias, a_norm_g[slot].reshape(1, v_w).astype(F32),
                       heads, dqk, dv)
            if converted:
                w_out16, w1_16, w2_16 = converted
            x = _out_proj_residual(y, w_out16, x, gate)
        else:
            assert not casts
            drnn = b_lam.shape[-1]
            x = _rglru_layer(x, scale, shift, gate, w_in16, w_out16,
                             b_conv_w[slot], b_conv_b[slot].reshape(1, drnn),
                             (0.5 * b_w_ra[slot]).astype(BF16), 0.5 * b_b_ra[slot].reshape(1, drnn),
                             (0.5 * b_w_ri[slot]).astype(BF16), 0.5 * b_b_ri[slot].reshape(1, drnn),
                             b_lam[slot].reshape(1, drnn))
        scale, shift, gate = site(layer, 1)
        last = layer == depth - 1
        x, converted = _mlp(x, scale, shift, gate, w1_16, w2_16, final_row, final_norm=last,
                            casts=() if last else big_weights(layer + 1))
        if converted:
            w_in16, w_out16, w1_16, w2_16 = converted
    return x
```

```python
import functools

import jax
import jax.numpy as jnp
from jax import lax
from jax.experimental import pallas as pl
from jax.experimental.pallas import tpu as pltpu

EPS = 1e-6
RG_C = 8.0
CONV_W = 4
LSTM_CHUNK = 128
V7X_LANES = 128
V7X_SUBLANES = 8
VMEM_LIMIT = 56 * 1024 * 1024

F32 = jnp.float32
BF16 = jnp.bfloat16


def _params(sem):
    return pltpu.CompilerParams(dimension_semantics=sem, vmem_limit_bytes=VMEM_LIMIT)


def _mod_norm(x, scale, shift):
    ms = jnp.mean(x * x, axis=-1, keepdims=True)
    return (x * lax.rsqrt(ms + EPS)) * (1.0 + scale) + shift


BF16_SUBLANES = 16


def _cast_block(r, c, n_steps):
    col_blocks = 1
    while r * col_blocks < n_steps * BF16_SUBLANES:
        col_blocks *= 2
    assert (r * col_blocks) % n_steps == 0 and c % (col_blocks * V7X_LANES) == 0, (r, c, n_steps)
    return (r * col_blocks // n_steps, c // col_blocks), col_blocks


def _cast_specs(casts, n_steps, step_id):
    in_specs, out_specs, out_shapes = [], [], []
    for stacked, lead, cols in casts:
        r = stacked.shape[1]
        block, cb = _cast_block(r, cols, n_steps)
        src_index = lambda *ids, cb=cb, lead=lead: (lead, step_id(*ids) // cb, step_id(*ids) % cb)
        dst_index = lambda *ids, cb=cb: (step_id(*ids) // cb, step_id(*ids) % cb)
        in_specs.append(pl.BlockSpec((None,) + block, src_index))
        out_specs.append(pl.BlockSpec(block, dst_index))
        out_shapes.append(jax.ShapeDtypeStruct((r, cols), BF16))
    return in_specs, out_specs, out_shapes


def _run_casts(src_refs, dst_refs):
    for src, dst in zip(src_refs, dst_refs):
        dst[...] = src[...].astype(dst.dtype)


def _ada_kernel(c_ref, w_ref, b_ref, o_ref):
    c = c_ref[...]
    s = (c * jax.nn.sigmoid(c)).astype(BF16)
    o_ref[0] = jnp.dot(s, w_ref[0].astype(BF16), preferred_element_type=F32) + b_ref[0]


def _ada_mod(c, ada_w, ada_b, tn=1024):
    depth, two, d, n = ada_w.shape
    sites = depth * two
    b = c.shape[0]
    bp = -(-b // V7X_SUBLANES) * V7X_SUBLANES
    c_pad = jnp.pad(c, ((0, bp - b), (0, 0)))
    w = ada_w.reshape(sites, d, n)
    bias = ada_b.reshape(sites, 1, n)
    out = pl.pallas_call(
        _ada_kernel,
        grid=(sites, n // tn),
        in_specs=[
            pl.BlockSpec((bp, d), lambda s, j: (0, 0)),
            pl.BlockSpec((1, d, tn), lambda s, j: (s, 0, j)),
            pl.BlockSpec((1, 1, tn), lambda s, j: (s, 0, j)),
        ],
        out_specs=pl.BlockSpec((1, bp, tn), lambda s, j: (s, 0, j)),
        out_shape=jax.ShapeDtypeStruct((sites, bp, n), F32),
        compiler_params=_params(("parallel", "parallel")),
        name="ada_mod",
    )(c_pad, w, bias)
    return out[:, :b, :]


def _norm_matmul_kernel(*refs, has_gates, n_cast, cast_j):
    n_in = 4 + has_gates
    x_ref, sc_ref, sh_ref, w_ref = refs[:4]
    cast_src = refs[n_in:n_in + n_cast]
    outs = refs[n_in + n_cast:-1]
    o_ref, cast_dst, h_ref = outs[0], outs[1 + has_gates:], refs[-1]
    j = pl.program_id(2)

    def project():
        o_ref[0] = jnp.dot(h_ref[...], w_ref[...], preferred_element_type=F32).astype(o_ref.dtype)

    @pl.when(j == 0)
    def _():
        h_ref[...] = _mod_norm(x_ref[0], sc_ref[0], sh_ref[0]).astype(BF16)
        if has_gates:
            outs[1][0] = jnp.dot(h_ref[...], refs[4][...], preferred_element_type=F32)
        project()

    pl.when(j > 0)(project)

    if n_cast:
        pl.when(j < cast_j)(lambda: _run_casts(cast_src, cast_dst))


def _norm_matmul(x, scale, shift, w, w_gates=None, casts=(), n_cols=None, tm=1024, tn=1024):
    b, s, d = x.shape
    n = w.shape[1] if n_cols is None else n_cols
    nj = n // tn
    grid = (b, s // tm, nj)
    cast_j = 1 << (nj.bit_length() - 1)
    step_id = lambda bi, i, j: (bi * (s // tm) + i) * cast_j + jnp.minimum(j, cast_j - 1)
    c_in, c_out, c_shapes = _cast_specs(casts, b * (s // tm) * cast_j, step_id)
    mod_spec = pl.BlockSpec((1, 1, d), lambda bi, i, j: (bi, 0, 0))
    in_specs = [pl.BlockSpec((1, tm, d), lambda bi, i, j: (bi, i, 0)), mod_spec, mod_spec,
                pl.BlockSpec((d, tn), lambda bi, i, j: (0, j))]
    out_specs = [pl.BlockSpec((1, tm, tn), lambda bi, i, j: (bi, i, j))]
    out_shapes = [jax.ShapeDtypeStruct((b, s, n), BF16)]
    args = [x, scale, shift, w]
    if w_gates is not None:
        ng = w_gates.shape[1]
        in_specs.append(pl.BlockSpec((d, ng), lambda bi, i, j: (0, 0)))
        out_specs.append(pl.BlockSpec((1, tm, ng), lambda bi, i, j: (bi, i, 0)))
        out_shapes.append(jax.ShapeDtypeStruct((b, s, ng), F32))
        args.append(w_gates)
    kernel = functools.partial(_norm_matmul_kernel, has_gates=w_gates is not None,
                               n_cast=len(casts), cast_j=cast_j)
    return pl.pallas_call(
        kernel, grid=grid,
        in_specs=in_specs + c_in, out_specs=out_specs + c_out, out_shape=out_shapes + c_shapes,
        scratch_shapes=[pltpu.VMEM((tm, d), BF16)],
        compiler_params=_params(("parallel", "parallel", "arbitrary")),
        name="norm_matmul_gates" if w_gates is not None else "norm_matmul",
    )(*args, *[stacked for stacked, _, _ in casts])


def _out_proj_kernel(y_ref, w_ref, x_ref, g_ref, o_ref):
    acc = jnp.dot(y_ref[0], w_ref[...], preferred_element_type=F32)
    o_ref[0] = x_ref[0] + g_ref[0] * acc


def _out_proj_residual(y, w, x, gate, tm=512, tn=2048):
    b, s, k = y.shape
    n = w.shape[1]
    return pl.pallas_call(
        _out_proj_kernel, grid=(b, s // tm, n // tn),
        in_specs=[
            pl.BlockSpec((1, tm, k), lambda bi, i, j: (bi, i, 0)),
            pl.BlockSpec((k, tn), lambda bi, i, j: (0, j)),
            pl.BlockSpec((1, tm, tn), lambda bi, i, j: (bi, i, j)),
            pl.BlockSpec((1, 1, tn), lambda bi, i, j: (bi, 0, j)),
        ],
        out_specs=pl.BlockSpec((1, tm, tn), lambda bi, i, j: (bi, i, j)),
        out_shape=jax.ShapeDtypeStruct((b, s, n), F32),
        compiler_params=_params(("parallel", "parallel", "arbitrary")),
        name="out_proj_residual",
    )(y, w, x, gate)


def _mlp_kernel(*refs, final_norm, n_cast):
    x_ref, sc_ref, sh_ref, g_ref, w1_ref, w2_ref, fg_ref = refs[:7]
    cast_src = refs[7:7 + n_cast]
    o_ref, cast_dst, h_ref = refs[7 + n_cast], refs[8 + n_cast:-1], refs[-1]
    f = pl.program_id(2)

    def ffn_slice():
        u = jnp.dot(h_ref[...], w1_ref[...], preferred_element_type=F32)
        u = jnp.square(jnp.maximum(u, 0.0)).astype(BF16)
        return g_ref[0] * jnp.dot(u, w2_ref[...], preferred_element_type=F32)

    @pl.when(f == 0)
    def _():
        h_ref[...] = _mod_norm(x_ref[0], sc_ref[0], sh_ref[0]).astype(BF16)
        o_ref[0] = x_ref[0] + ffn_slice()

    @pl.when(f > 0)
    def _():
        o_ref[0] += ffn_slice()

    _run_casts(cast_src, cast_dst)

    if final_norm:
        @pl.when(f == pl.num_programs(2) - 1)
        def _():
            y = o_ref[0]
            ms = jnp.mean(y * y, axis=-1, keepdims=True)
            o_ref[0] = (y * lax.rsqrt(ms + EPS)) * fg_ref[...]


def _mlp(x, scale, shift, gate, w1, w2, final_g, final_norm, casts=(), tm=1024, tf=512):
    b, s, d = x.shape
    dff = w1.shape[1]
    ni, nf = s // tm, dff // tf
    c_in, c_out, c_shapes = _cast_specs(casts, b * ni * nf, lambda bi, i, f: (bi * ni + i) * nf + f)
    mod_spec = pl.BlockSpec((1, 1, d), lambda bi, i, f: (bi, 0, 0))
    out = pl.pallas_call(
        functools.partial(_mlp_kernel, final_norm=final_norm, n_cast=len(casts)),
        grid=(b, ni, nf),
        in_specs=[
            pl.BlockSpec((1, tm, d), lambda bi, i, f: (bi, i, 0)),
            mod_spec, mod_spec, mod_spec,
            pl.BlockSpec((d, tf), lambda bi, i, f: (0, f)),
            pl.BlockSpec((tf, d), lambda bi, i, f: (f, 0)),
            pl.BlockSpec((1, d), lambda bi, i, f: (0, 0)),
        ] + c_in,
        out_specs=[pl.BlockSpec((1, tm, d), lambda bi, i, f: (bi, i, 0))] + c_out,
        out_shape=[jax.ShapeDtypeStruct((b, s, d), F32)] + c_shapes,
        scratch_shapes=[pltpu.VMEM((tm, d), BF16)],
        compiler_params=_params(("parallel", "parallel", "arbitrary")),
        name="mlp_final" if final_norm else "mlp",
    )(x, scale, shift, gate, w1, w2, final_g, *[stacked for stacked, _, _ in casts])
    return out[0], out[1:]


def _cumsum_rows(x, tril):
    x1 = x.astype(BF16)
    r1 = x - x1.astype(F32)
    x2 = r1.astype(BF16)
    x3 = (r1 - x2.astype(F32)).astype(BF16)
    d = lambda t: jnp.dot(tril, t, preferred_element_type=F32)
    return d(x1) + d(x2) + d(x3)


def _cummax_rows(x):
    n = x.shape[0]
    row = lax.broadcasted_iota(jnp.int32, x.shape, 0)
    sh = 1
    while sh < n:
        if sh < V7X_SUBLANES:
            prev = jnp.where(row >= sh, pltpu.roll(x, sh, 0), -jnp.inf)
        else:
            prev = jnp.concatenate([jnp.full((sh, x.shape[1]), -jnp.inf, x.dtype), x[:n - sh]], axis=0)
        x = jnp.maximum(x, prev)
        sh *= 2
    return x


def _log_sigmoid(z):
    return jnp.minimum(z, 0.0) - jnp.log1p(jnp.exp(-jnp.abs(z)))


def _mlstm_kernel(q_ref, k_ref, v_ref, o_ref, g_ref, bias_ref, ng_ref, y_ref, cn_ref, m_ref,
                  *, rows, heads, dqk, dv):
    L, LN = LSTM_CHUNK, V7X_LANES

    @pl.when(pl.program_id(1) == 0)
    def _():
        cn_ref[...] = jnp.zeros_like(cn_ref)
        m_ref[...] = jnp.zeros_like(m_ref)

    row = lax.broadcasted_iota(jnp.int32, (L, L), 0)
    col = lax.broadcasted_iota(jnp.int32, (L, L), 1)
    causal = row >= col
    tril = jnp.where(causal, 1.0, 0.0).astype(BF16)
    eye = jnp.where(lax.broadcasted_iota(jnp.int32, (dqk, dqk), 0)
                    == lax.broadcasted_iota(jnp.int32, (dqk, dqk), 1), 1.0, 0.0).astype(BF16)

    b, cm, m_row, decay_row, r_rows, wk_rows = [], [], [], [], [], []
    for bb in range(rows):
        log_in = g_ref[bb, :, 0:LN] + bias_ref[:, 0:LN]
        b.append(_cumsum_rows(_log_sigmoid(g_ref[bb, :, LN:2 * LN] + bias_ref[:, LN:2 * LN]), tril))
        r = log_in - b[bb]
        cm.append(_cummax_rows(r))
        m_row.append(m_ref[bb:bb + 1, :])
        mx_last = jnp.maximum(m_row[bb], cm[bb][L - 1:L, :])
        decay_row.append(jnp.exp(m_row[bb] - mx_last))
        wk = jnp.exp(r - mx_last)
        m_ref[bb:bb + 1, :] = b[bb][L - 1:L, :] + mx_last
        r_rows.append(r.T)
        wk_rows.append(wk.T)

    floor_scale = float(dqk) ** 0.5
    ones_blk = jnp.ones((L, LN), BF16)
    nt = (((1,), (1,)), ((), ()))
    units = [(bb, h) for bb in range(rows) for h in range(heads)]
    ur = range(len(units))
    q = [q_ref[bb, :, h * dqk:(h + 1) * dqk] for bb, h in units]
    k = [k_ref[bb, :, h * dqk:(h + 1) * dqk] for bb, h in units]
    s = [lax.dot_general(q[u], k[u], nt, preferred_element_type=F32) for u in ur]
    k_t = [lax.dot_general(eye, k[u], nt, preferred_element_type=F32) for u in ur]

    lhs, rhs, vo, floor = [], [], [], []
    for u, (bb, h) in enumerate(units):
        m_h = m_row[bb][:, h:h + 1]
        mx = jnp.maximum(jnp.broadcast_to(cm[bb][:, h:h + 1], (L, LN)), m_h)
        b_t = jnp.broadcast_to(b[bb][:, h:h + 1], (L, LN))
        p = jnp.exp(jnp.where(causal, r_rows[bb][h:h + 1, :] - mx, -jnp.inf)) * s[u]
        wq = (q[u].astype(F32) * jnp.exp(m_h - mx)).astype(BF16)
        lhs.append(jnp.concatenate([wq, p.astype(BF16)], axis=1))
        vo.append(jnp.concatenate([v_ref[bb, :, h * dv:(h + 1) * dv], ones_blk], axis=1))
        rhs.append(jnp.concatenate([cn_ref[u].astype(BF16), vo[u]], axis=0))
        floor.append(jnp.exp(-(b_t + mx)) * floor_scale)

    out = [jnp.dot(lhs[u], rhs[u], preferred_element_type=F32) for u in ur]

    for u, (bb, h) in enumerate(units):
        kw_t = (k_t[u] * wk_rows[bb][h:h + 1, :]).astype(BF16)
        cn_ref[u] = (decay_row[bb][:, h:h + 1] * cn_ref[u]
                     + jnp.dot(kw_t, vo[u], preferred_element_type=F32))

    for u, (bb, h) in enumerate(units):
        inv = 1.0 / jnp.maximum(jnp.abs(out[u][:, dv:]), floor[u])
        hh = out[u][:, :dv] * jnp.concatenate([inv] * (dv // LN), axis=1)
        ms = jnp.mean(hh * hh, axis=1, keepdims=True)
        hn = (hh * lax.rsqrt(ms + EPS)) * ng_ref[:, h * dv:(h + 1) * dv]
        og = 0.5 * jnp.tanh(0.5 * o_ref[bb, :, h * dv:(h + 1) * dv].astype(F32)) + 0.5
        y_ref[bb, :, h * dv:(h + 1) * dv] = (hn * og).astype(y_ref.dtype)


def _mlstm(proj, gates, gate_bias, norm_g, heads, dqk, dv):
    b, s, _ = proj.shape
    L = LSTM_CHUNK
    qk_w, v_w = heads * dqk, heads * dv
    rows = 4 if b % 4 == 0 else 1
    assert v_w == 2 * qk_w and dqk == L and heads <= V7X_LANES and rows <= V7X_SUBLANES
    kernel = functools.partial(_mlstm_kernel, rows=rows, heads=heads, dqk=dqk, dv=dv)
    return pl.pallas_call(
        kernel, grid=(b // rows, s // L),
        in_specs=[
            pl.BlockSpec((rows, L, qk_w), lambda bi, c: (bi, c, 0)),
            pl.BlockSpec((rows, L, qk_w), lambda bi, c: (bi, c, 1)),
            pl.BlockSpec((rows, L, v_w), lambda bi, c: (bi, c, 1)),
            pl.BlockSpec((rows, L, v_w), lambda bi, c: (bi, c, 2)),
            pl.BlockSpec((rows, L, gates.shape[2]), lambda bi, c: (bi, c, 0)),
            pl.BlockSpec((1, gate_bias.shape[1]), lambda bi, c: (0, 0)),
            pl.BlockSpec((1, v_w), lambda bi, c: (0, 0)),
        ],
        out_specs=pl.BlockSpec((rows, L, v_w), lambda bi, c: (bi, c, 0)),
        out_shape=jax.ShapeDtypeStruct((b, s, v_w), BF16),
        scratch_shapes=[pltpu.VMEM((rows * heads, dqk, dv + V7X_LANES), F32),
                        pltpu.VMEM((V7X_SUBLANES, V7X_LANES), F32)],
        compiler_params=_params(("parallel", "arbitrary")),
        name="mlstm",
    )(proj, proj, proj, proj, gates, gate_bias, norm_g)


def _scan_permutation(tt):
    seg = tt // V7X_SUBLANES
    r = jnp.arange(tt)
    src = (r % V7X_SUBLANES) * seg + r // V7X_SUBLANES
    to_interleaved = (src[:, None] == jnp.arange(tt)[None, :]).astype(BF16)
    return to_interleaved, to_interleaved.T


def _sqrt_nonneg(x):
    return jnp.where(x > 0.0, x * lax.rsqrt(x), 0.0)


def _gelu_tanh(x):
    c = (2.0 / jnp.pi) ** 0.5
    hx = 0.5 * x
    return hx + hx * jnp.tanh(x * (c + (c * 0.044715) * (x * x)))


def _rglru_layer_kernel(x_ref, sc_ref, sh_ref, g_ref, win_ref, wout_ref, pm_ref, pmt_ref, cw_ref, cb_ref,
                        wra_ref, bra_ref, wri_ref, bri_ref, lam_ref, o_ref,
                        hp_ref, y_ref, gp_ref, a_ref, u_ref, tail_ref, carry_ref, *, tt, nblk, bw):
    S8 = V7X_SUBLANES
    seg = tt // S8
    H = CONV_W - 1
    d = a_ref.shape[1]

    @pl.when(pl.program_id(1) == 0)
    def _():
        tail_ref[...] = jnp.zeros_like(tail_ref)
        carry_ref[...] = jnp.zeros_like(carry_ref)

    h = _mod_norm(x_ref[0], sc_ref[0], sh_ref[0]).astype(BF16)
    hp_ref[...] = jnp.dot(pm_ref[...], h, preferred_element_type=F32).astype(BF16)

    lam = lam_ref[...]
    half_nsp = (-0.5 * RG_C) * (jnp.maximum(-lam, 0.0) + jnp.log1p(jnp.exp(-jnp.abs(lam))))
    first_segment = lax.broadcasted_iota(jnp.int32, (S8, bw), 0) == 0

    def in_proj(first_col, group):
        cols = slice(first_col + group * GROUP * bw, first_col + (group + 1) * GROUP * bw)
        return jnp.dot(hp_ref[...], win_ref[:, cols], preferred_element_type=F32)

    def project_gate(group):
        gp_ref[:, group * GROUP * bw:(group + 1) * GROUP * bw] = in_proj(d, group)

    GROUP = 2
    early_gate_groups = nblk // GROUP // 2
    ahead = in_proj(0, 0)
    for nb in range(nblk):
        sl = slice(nb * bw, (nb + 1) * bw)
        if nb % GROUP == 0:
            group_out = ahead
            if nb + GROUP < nblk:
                ahead = in_proj(0, nb // GROUP + 1)
            if nb // GROUP < early_gate_groups:
                project_gate(nb // GROUP)
        xp = group_out[:, (nb % GROUP) * bw:(nb % GROUP + 1) * bw]
        hist = []
        for g in range(H):
            cur = xp[(seg - H + g) * S8:(seg - H + g + 1) * S8, :]
            prev = tail_ref[g * S8:(g + 1) * S8, sl]
            hist.append(jnp.where(first_segment, pltpu.roll(prev, 1, 0), pltpu.roll(cur, 1, 0)))
        tail_ref[:, sl] = xp[(seg - H) * S8:, :]
        xext = jnp.concatenate(hist + [xp], axis=0)
        xc = cb_ref[:, sl] + cw_ref[H:H + 1, sl] * xp
        for k in range(H):
            xc = xc + cw_ref[k:k + 1, sl] * xext[k * S8:k * S8 + tt, :]

        xc16 = xc.astype(BF16)
        t_r = jnp.tanh(jnp.dot(xc16, wra_ref[nb], preferred_element_type=F32) + bra_ref[:, sl])
        i = 0.5 * jnp.tanh(jnp.dot(xc16, wri_ref[nb], preferred_element_type=F32) + bri_ref[:, sl]) + 0.5
        log_a = t_r * half_nsp[:, sl] + half_nsp[:, sl]
        a = jnp.exp(log_a)
        one_m_a2 = jnp.tanh(log_a) * (-1.0 - a * a)
        a_ref[:, sl] = a
        u_ref[:, sl] = _sqrt_nonneg(one_m_a2) * (i * xc)

    for group in range(early_gate_groups, nblk // GROUP):
        project_gate(group)
    h_end, p_end = jnp.zeros((S8, d), F32), jnp.ones((S8, d), F32)
    for j in range(seg):
        rows = slice(j * S8, (j + 1) * S8)
        a = a_ref[rows, :]
        h_end = a * h_end + u_ref[rows, :]
        p_end = a * p_end
        u_ref[rows, :] = h_end
        a_ref[rows, :] = p_end

    c = carry_ref[0:1, :]
    carry_rows = []
    for sgi in range(S8):
        carry_rows.append(c)
        c = h_end[sgi:sgi + 1, :] + p_end[sgi:sgi + 1, :] * c
    carry_ref[0:1, :] = c
    carry_in = jnp.concatenate(carry_rows, axis=0)

    pmt = pmt_ref[...]

    def out_proj(group):
        gsl = slice(group * GROUP * bw, (group + 1) * GROUP * bw)
        part = g_ref[0] * jnp.dot(y_ref[:, gsl], wout_ref[gsl, :], preferred_element_type=F32)
        if group == 0:
            o_ref[0] = x_ref[0] + part
        else:
            o_ref[0] += part

    for nb in range(nblk):
        sl = slice(nb * bw, (nb + 1) * bw)
        cin = jnp.concatenate([carry_in[:, sl]] * seg, axis=0)
        hs = u_ref[:, sl] + a_ref[:, sl] * cin
        yi = (hs * _gelu_tanh(gp_ref[:, sl])).astype(BF16)
        y_ref[:, sl] = jnp.dot(pmt, yi, preferred_element_type=F32).astype(BF16)
        if nb % GROUP == GROUP - 1 and nb >= 2 * GROUP - 1:
            out_proj(nb // GROUP - 1)
    out_proj(nblk // GROUP - 1)


def _rglru_layer(x, scale, shift, gate, w_in, w_out, conv_w, conv_b, w_ra, b_ra, w_ri, b_ri, lam, tt=256):
    b, s, dm = x.shape
    d = w_out.shape[0]
    nblk, bw, _ = w_ra.shape
    pm, pmt = _scan_permutation(tt)
    const = lambda shape: pl.BlockSpec(shape, lambda bi, i: (0,) * len(shape))
    resident = lambda shape: pl.BlockSpec(shape, lambda bi, i: (0,) * len(shape),
                                          pipeline_mode=pl.Buffered(1))
    mod_spec = pl.BlockSpec((1, 1, dm), lambda bi, i: (bi, 0, 0))
    kernel = functools.partial(_rglru_layer_kernel, tt=tt, nblk=nblk, bw=bw)
    return pl.pallas_call(
        kernel, grid=(b, s // tt),
        in_specs=[
            pl.BlockSpec((1, tt, dm), lambda bi, i: (bi, i, 0)),
            mod_spec, mod_spec, mod_spec,
            resident((dm, 2 * d)), resident((d, dm)),
            const((tt, tt)), const((tt, tt)),
            const((CONV_W, d)), const((1, d)),
            const((nblk, bw, bw)), const((1, d)), const((nblk, bw, bw)), const((1, d)),
            const((1, d)),
        ],
        out_specs=pl.BlockSpec((1, tt, dm), lambda bi, i: (bi, i, 0)),
        out_shape=jax.ShapeDtypeStruct((b, s, dm), F32),
        scratch_shapes=[pltpu.VMEM((tt, dm), BF16),
                        pltpu.VMEM((tt, d), BF16),
                        pltpu.VMEM((tt, d), F32),
                        pltpu.VMEM((tt, d), F32),
                        pltpu.VMEM((tt, d), F32),
                        pltpu.VMEM(((CONV_W - 1) * V7X_SUBLANES, d), F32),
                        pltpu.VMEM((V7X_SUBLANES, d), F32)],
        compiler_params=_params(("parallel", "arbitrary")),
        name="rglru_layer",
    )(x, scale, shift, gate, w_in, w_out, pm, pmt, conv_w, conv_b, w_ra, b_ra, w_ri, b_ri, lam)


def _gate_lanes(ig, fg):
    pad = [(0, 0)] * (ig.ndim - 1) + [(0, V7X_LANES - ig.shape[-1])]
    return jnp.concatenate([jnp.pad(ig, pad), jnp.pad(fg, pad)], axis=-1)


def kernel(x, c, ada_w, ada_b, a_w_in, a_b_gate, a_norm_g, a_w_out, b_w_in, b_conv_w, b_conv_b,
           b_w_ra, b_b_ra, b_w_ri, b_b_ri, b_lam, b_w_out, mlp_w1, mlp_w2, final_g):
    depth = ada_w.shape[0]
    d = x.shape[-1]
    heads = a_b_gate.shape[-1]
    v_w = a_norm_g.shape[-1]
    qk_w = (a_w_in.shape[-1] - 2 * v_w - 2 * heads) // 2
    dqk, dv = qk_w // heads, v_w // heads
    n_main = 2 * qk_w + 2 * v_w

    mod = _ada_mod(c, ada_w, ada_b)
    final_row = final_g.reshape(1, d)

    def site(layer, j):
        m = mod[layer * 2 + j]
        return (m[:, None, d:2 * d], m[:, None, 0:d], m[:, None, 2 * d:3 * d])

    def big_weights(layer):
        slot = layer // 2
        mixer = ((a_w_in, slot, n_main), (a_w_out, slot, d)) if layer % 2 == 0 else \
                ((b_w_in, slot, b_w_in.shape[-1]), (b_w_out, slot, d))
        return mixer + ((mlp_w1, layer, mlp_w1.shape[-1]), (mlp_w2, layer, d))

    first, lead, _ = big_weights(0)[0]
    w_in16 = first[lead].astype(BF16)
    for layer in range(depth):
        slot = layer // 2
        casts = big_weights(layer)[1:] if layer == 0 else ()
        scale, shift, gate = site(layer, 0)
        if layer % 2 == 0:
            w_g = a_w_in[slot][:, n_main:]
            w_gates = _gate_lanes(w_g[:, :heads], w_g[:, heads:]).astype(BF16)
            proj, gates, *converted = _norm_matmul(x, scale, shift, w_in16, w_gates, casts=casts,
                                                   n_cols=n_main)
            gate_bias = _gate_lanes(a_b_gate[slot, 0:1].astype(F32), a_b_gate[slot, 1:2].astype(F32))
            y = _mlstm(proj, gates, gate_bias, a_norm_g[slot].reshape(1, v_w).astype(F32),
                       heads, dqk, dv)
            if converted:
                w_out16, w1_16, w2_16 = converted
            x = _out_proj_residual(y, w_out16, x, gate)
        else:
            assert not casts
            drnn = b_lam.shape[-1]
            x = _rglru_layer(x, scale, shift, gate, w_in16, w_out16,
                             b_conv_w[slot], b_conv_b[slot].reshape(1, drnn),
                             (0.5 * b_w_ra[slot]).astype(BF16), 0.5 * b_b_ra[slot].reshape(1, drnn),
                             (0.5 * b_w_ri[slot]).astype(BF16), 0.5 * b_b_ri[slot].reshape(1, drnn),
                             b_lam[slot].reshape(1, drnn))
        scale, shift, gate = site(layer, 1)
        last = layer == depth - 1
        x, converted = _mlp(x, scale, shift, gate, w1_16, w2_16, final_row, final_norm=last,
                            casts=() if last else big_weights(layer + 1))
        if converted:
            w_in16, w_out16, w1_16, w2_16 = converted
    return x
```

```python
import functools

import jax
import jax.numpy as jnp
from jax import lax
from jax.experimental import pallas as pl
from jax.experimental.pallas import tpu as pltpu

EPS = 1e-6
RG_C = 8.0
CONV_W = 4
LSTM_CHUNK = 128
V7X_LANES = 128
V7X_SUBLANES = 8
VMEM_LIMIT = 60 * 1024 * 1024

F32 = jnp.float32
BF16 = jnp.bfloat16


def _params(sem):
    return pltpu.CompilerParams(dimension_semantics=sem, vmem_limit_bytes=VMEM_LIMIT)


def _mod_norm(x, scale, shift):
    ms = jnp.mean(x * x, axis=-1, keepdims=True)
    return (x * lax.rsqrt(ms + EPS)) * (1.0 + scale) + shift


BF16_SUBLANES = 16


def _cast_block(r, c, n_steps):
    col_blocks = 1
    while r * col_blocks < n_steps * BF16_SUBLANES:
        col_blocks *= 2
    assert (r * col_blocks) % n_steps == 0 and c % (col_blocks * V7X_LANES) == 0, (r, c, n_steps)
    return (r * col_blocks // n_steps, c // col_blocks), col_blocks


def _cast_specs(casts, n_steps, step_id):
    in_specs, out_specs, out_shapes = [], [], []
    for stacked, lead, cols in casts:
        r = stacked.shape[1]
        block, cb = _cast_block(r, cols, n_steps)
        src_index = lambda *ids, cb=cb, lead=lead: (lead, step_id(*ids) // cb, step_id(*ids) % cb)
        dst_index = lambda *ids, cb=cb: (step_id(*ids) // cb, step_id(*ids) % cb)
        in_specs.append(pl.BlockSpec((None,) + block, src_index))
        out_specs.append(pl.BlockSpec(block, dst_index))
        out_shapes.append(jax.ShapeDtypeStruct((r, cols), BF16))
    return in_specs, out_specs, out_shapes


def _run_casts(src_refs, dst_refs):
    for src, dst in zip(src_refs, dst_refs):
        dst[...] = src[...].astype(dst.dtype)


def _ada_kernel(c_ref, w_ref, b_ref, o_ref):
    c = c_ref[...]
    s = (c * jax.nn.sigmoid(c)).astype(BF16)
    o_ref[0] = jnp.dot(s, w_ref[0].astype(BF16), preferred_element_type=F32) + b_ref[0]


def _ada_mod(c, ada_w, ada_b, tn=1024):
    depth, two, d, n = ada_w.shape
    sites = depth * two
    b = c.shape[0]
    bp = -(-b // V7X_SUBLANES) * V7X_SUBLANES
    c_pad = jnp.pad(c, ((0, bp - b), (0, 0)))
    w = ada_w.reshape(sites, d, n)
    bias = ada_b.reshape(sites, 1, n)
    out = pl.pallas_call(
        _ada_kernel,
        grid=(sites, n // tn),
        in_specs=[
            pl.BlockSpec((bp, d), lambda s, j: (0, 0)),
            pl.BlockSpec((1, d, tn), lambda s, j: (s, 0, j)),
            pl.BlockSpec((1, 1, tn), lambda s, j: (s, 0, j)),
        ],
        out_specs=pl.BlockSpec((1, bp, tn), lambda s, j: (s, 0, j)),
        out_shape=jax.ShapeDtypeStruct((sites, bp, n), F32),
        compiler_params=_params(("parallel", "parallel")),
        name="ada_mod",
    )(c_pad, w, bias)
    return out[:, :b, :]


def _norm_matmul_kernel(*refs, has_gates, n_cast, cast_j):
    n_in = 4 + has_gates
    x_ref, sc_ref, sh_ref, w_ref = refs[:4]
    cast_src = refs[n_in:n_in + n_cast]
    outs = refs[n_in + n_cast:-1]
    o_ref, cast_dst, h_ref = outs[0], outs[1 + has_gates:], refs[-1]
    j = pl.program_id(2)

    def project():
        o_ref[0] = jnp.dot(h_ref[...], w_ref[...], preferred_element_type=F32).astype(o_ref.dtype)

    @pl.when(j == 0)
    def _():
        h_ref[...] = _mod_norm(x_ref[0], sc_ref[0], sh_ref[0]).astype(BF16)
        if has_gates:
            outs[1][0] = jnp.dot(h_ref[...], refs[4][...], preferred_element_type=F32)
        project()

    pl.when(j > 0)(project)

    if n_cast:
        pl.when(j < cast_j)(lambda: _run_casts(cast_src, cast_dst))


def _norm_matmul(x, scale, shift, w, w_gates=None, casts=(), n_cols=None, tm=1024, tn=1024):
    b, s, d = x.shape
    n = w.shape[1] if n_cols is None else n_cols
    nj = n // tn
    grid = (b, s // tm, nj)
    cast_j = 1 << (nj.bit_length() - 1)
    step_id = lambda bi, i, j: (bi * (s // tm) + i) * cast_j + jnp.minimum(j, cast_j - 1)
    c_in, c_out, c_shapes = _cast_specs(casts, b * (s // tm) * cast_j, step_id)
    mod_spec = pl.BlockSpec((1, 1, d), lambda bi, i, j: (bi, 0, 0))
    in_specs = [pl.BlockSpec((1, tm, d), lambda bi, i, j: (bi, i, 0)), mod_spec, mod_spec,
                pl.BlockSpec((d, tn), lambda bi, i, j: (0, j))]
    out_specs = [pl.BlockSpec((1, tm, tn), lambda bi, i, j: (bi, i, j))]
    out_shapes = [jax.ShapeDtypeStruct((b, s, n), BF16)]
    args = [x, scale, shift, w]
    if w_gates is not None:
        ng = w_gates.shape[1]
        in_specs.append(pl.BlockSpec((d, ng), lambda bi, i, j: (0, 0)))
        out_specs.append(pl.BlockSpec((1, tm, ng), lambda bi, i, j: (bi, i, 0)))
        out_shapes.append(jax.ShapeDtypeStruct((b, s, ng), F32))
        args.append(w_gates)
    kernel = functools.partial(_norm_matmul_kernel, has_gates=w_gates is not None,
                               n_cast=len(casts), cast_j=cast_j)
    return pl.pallas_call(
        kernel, grid=grid,
        in_specs=in_specs + c_in, out_specs=out_specs + c_out, out_shape=out_shapes + c_shapes,
        scratch_shapes=[pltpu.VMEM((tm, d), BF16)],
        compiler_params=_params(("parallel", "parallel", "arbitrary")),
        name="norm_matmul_gates" if w_gates is not None else "norm_matmul",
    )(*args, *[stacked for stacked, _, _ in casts])


def _out_proj_kernel(y_ref, w_ref, x_ref, g_ref, o_ref):
    acc = jnp.dot(y_ref[0], w_ref[...], preferred_element_type=F32)
    o_ref[0] = x_ref[0] + g_ref[0] * acc


def _out_proj_residual(y, w, x, gate, tm=512, tn=2048):
    b, s, k = y.shape
    n = w.shape[1]
    return pl.pallas_call(
        _out_proj_kernel, grid=(b, s // tm, n // tn),
        in_specs=[
            pl.BlockSpec((1, tm, k), lambda bi, i, j: (bi, i, 0)),
            pl.BlockSpec((k, tn), lambda bi, i, j: (0, j)),
            pl.BlockSpec((1, tm, tn), lambda bi, i, j: (bi, i, j)),
            pl.BlockSpec((1, 1, tn), lambda bi, i, j: (bi, 0, j)),
        ],
        out_specs=pl.BlockSpec((1, tm, tn), lambda bi, i, j: (bi, i, j)),
        out_shape=jax.ShapeDtypeStruct((b, s, n), F32),
        compiler_params=_params(("parallel", "parallel", "arbitrary")),
        name="out_proj_residual",
    )(y, w, x, gate)


def _mlp_kernel(*refs, final_norm, n_cast):
    x_ref, sc_ref, sh_ref, g_ref, w1_ref, w2_ref, fg_ref = refs[:7]
    cast_src = refs[7:7 + n_cast]
    o_ref, cast_dst, h_ref = refs[7 + n_cast], refs[8 + n_cast:-1], refs[-1]
    f = pl.program_id(2)

    def ffn_slice():
        u = jnp.dot(h_ref[...], w1_ref[...], preferred_element_type=F32)
        u = jnp.square(jnp.maximum(u, 0.0)).astype(BF16)
        return g_ref[0] * jnp.dot(u, w2_ref[...], preferred_element_type=F32)

    @pl.when(f == 0)
    def _():
        h_ref[...] = _mod_norm(x_ref[0], sc_ref[0], sh_ref[0]).astype(BF16)
        o_ref[0] = x_ref[0] + ffn_slice()

    @pl.when(f > 0)
    def _():
        o_ref[0] += ffn_slice()

    _run_casts(cast_src, cast_dst)

    if final_norm:
        @pl.when(f == pl.num_programs(2) - 1)
        def _():
            y = o_ref[0]
            ms = jnp.mean(y * y, axis=-1, keepdims=True)
            o_ref[0] = (y * lax.rsqrt(ms + EPS)) * fg_ref[...]


def _mlp(x, scale, shift, gate, w1, w2, final_g, final_norm, casts=(), tm=1024, tf=None):
    b, s, d = x.shape
    dff = w1.shape[1]
    if tf is None:
        tf = 512 if casts else 1024
    ni, nf = s // tm, dff // tf
    c_in, c_out, c_shapes = _cast_specs(casts, b * ni * nf, lambda bi, i, f: (bi * ni + i) * nf + f)
    mod_spec = pl.BlockSpec((1, 1, d), lambda bi, i, f: (bi, 0, 0))
    out = pl.pallas_call(
        functools.partial(_mlp_kernel, final_norm=final_norm, n_cast=len(casts)),
        grid=(b, ni, nf),
        in_specs=[
            pl.BlockSpec((1, tm, d), lambda bi, i, f: (bi, i, 0)),
            mod_spec, mod_spec, mod_spec,
            pl.BlockSpec((d, tf), lambda bi, i, f: (0, f)),
            pl.BlockSpec((tf, d), lambda bi, i, f: (f, 0)),
            pl.BlockSpec((1, d), lambda bi, i, f: (0, 0)),
        ] + c_in,
        out_specs=[pl.BlockSpec((1, tm, d), lambda bi, i, f: (bi, i, 0))] + c_out,
        out_shape=[jax.ShapeDtypeStruct((b, s, d), F32)] + c_shapes,
        scratch_shapes=[pltpu.VMEM((tm, d), BF16)],
        compiler_params=_params(("parallel", "parallel", "arbitrary")),
        name="mlp_final" if final_norm else "mlp",
    )(x, scale, shift, gate, w1, w2, final_g, *[stacked for stacked, _, _ in casts])
    return out[0], out[1:]


def _cumsum_rows(x, tril):
    x1 = x.astype(BF16)
    r1 = x - x1.astype(F32)
    x2 = r1.astype(BF16)
    x3 = (r1 - x2.astype(F32)).astype(BF16)
    d = lambda t: jnp.dot(tril, t, preferred_element_type=F32)
    return d(x1) + d(x2) + d(x3)


def _cummax_rows(x):
    n = x.shape[0]
    row = lax.broadcasted_iota(jnp.int32, x.shape, 0)
    sh = 1
    while sh < n:
        if sh < V7X_SUBLANES:
            prev = jnp.where(row >= sh, pltpu.roll(x, sh, 0), -jnp.inf)
        else:
            prev = jnp.concatenate([jnp.full((sh, x.shape[1]), -jnp.inf, x.dtype), x[:n - sh]], axis=0)
        x = jnp.maximum(x, prev)
        sh *= 2
    return x


def _log_sigmoid(z):
    return jnp.minimum(z, 0.0) - jnp.log1p(jnp.exp(-jnp.abs(z)))


def _mlstm_kernel(q_ref, k_ref, v_ref, o_ref, g_ref, bias_ref, ng_ref, y_ref, cn_ref, m_ref,
                  *, rows, heads, dqk, dv):
    L, LN = LSTM_CHUNK, V7X_LANES

    @pl.when(pl.program_id(1) == 0)
    def _():
        cn_ref[...] = jnp.zeros_like(cn_ref)
        m_ref[...] = jnp.zeros_like(m_ref)

    row = lax.broadcasted_iota(jnp.int32, (L, L), 0)
    col = lax.broadcasted_iota(jnp.int32, (L, L), 1)
    causal = row >= col
    tril = jnp.where(causal, 1.0, 0.0).astype(BF16)
    eye = jnp.where(lax.broadcasted_iota(jnp.int32, (dqk, dqk), 0)
                    == lax.broadcasted_iota(jnp.int32, (dqk, dqk), 1), 1.0, 0.0).astype(BF16)

    b, cm, m_row, decay_row, r_rows, wk_rows = [], [], [], [], [], []
    for bb in range(rows):
        log_in = g_ref[bb, :, 0:LN] + bias_ref[:, 0:LN]
        b.append(_cumsum_rows(_log_sigmoid(g_ref[bb, :, LN:2 * LN] + bias_ref[:, LN:2 * LN]), tril))
        r = log_in - b[bb]
        cm.append(_cummax_rows(r))
        m_row.append(m_ref[bb:bb + 1, :])
        mx_last = jnp.maximum(m_row[bb], cm[bb][L - 1:L, :])
        decay_row.append(jnp.exp(m_row[bb] - mx_last))
        wk = jnp.exp(r - mx_last)
        m_ref[bb:bb + 1, :] = b[bb][L - 1:L, :] + mx_last
        r_rows.append(r.T)
        wk_rows.append(wk.T)

    floor_scale = float(dqk) ** 0.5
    ones_blk = jnp.ones((L, LN), BF16)
    nt = (((1,), (1,)), ((), ()))
    units = [(bb, h) for bb in range(rows) for h in range(heads)]
    ur = range(len(units))
    q = [q_ref[bb, :, h * dqk:(h + 1) * dqk] for bb, h in units]
    k = [k_ref[bb, :, h * dqk:(h + 1) * dqk] for bb, h in units]
    s = [lax.dot_general(q[u], k[u], nt, preferred_element_type=F32) for u in ur]
    k_t = [lax.dot_general(eye, k[u], nt, preferred_element_type=F32) for u in ur]

    lhs, rhs, vo, floor = [], [], [], []
    for u, (bb, h) in enumerate(units):
        m_h = m_row[bb][:, h:h + 1]
        mx = jnp.maximum(jnp.broadcast_to(cm[bb][:, h:h + 1], (L, LN)), m_h)
        b_t = jnp.broadcast_to(b[bb][:, h:h + 1], (L, LN))
        p = jnp.exp(jnp.where(causal, r_rows[bb][h:h + 1, :] - mx, -jnp.inf)) * s[u]
        wq = (q[u].astype(F32) * jnp.exp(m_h - mx)).astype(BF16)
        lhs.append(jnp.concatenate([wq, p.astype(BF16)], axis=1))
        vo.append(jnp.concatenate([v_ref[bb, :, h * dv:(h + 1) * dv], ones_blk], axis=1))
        rhs.append(jnp.concatenate([cn_ref[u].astype(BF16), vo[u]], axis=0))
        floor.append(jnp.exp(-(b_t + mx)) * floor_scale)

    out = [jnp.dot(lhs[u], rhs[u], preferred_element_type=F32) for u in ur]

    for u, (bb, h) in enumerate(units):
        kw_t = (k_t[u] * wk_rows[bb][h:h + 1, :]).astype(BF16)
        cn_ref[u] = (decay_row[bb][:, h:h + 1] * cn_ref[u]
                     + jnp.dot(kw_t, vo[u], preferred_element_type=F32))

    for u, (bb, h) in enumerate(units):
        inv = 1.0 / jnp.maximum(jnp.abs(out[u][:, dv:]), floor[u])
        hh = out[u][:, :dv] * jnp.concatenate([inv] * (dv // LN), axis=1)
        ms = jnp.mean(hh * hh, axis=1, keepdims=True)
        hn = (hh * lax.rsqrt(ms + EPS)) * ng_ref[:, h * dv:(h + 1) * dv]
        og = 0.5 * jnp.tanh(0.5 * o_ref[bb, :, h * dv:(h + 1) * dv].astype(F32)) + 0.5
        y_ref[bb, :, h * dv:(h + 1) * dv] = (hn * og).astype(y_ref.dtype)


def _mlstm(proj, gates, gate_bias, norm_g, heads, dqk, dv):
    b, s, _ = proj.shape
    L = LSTM_CHUNK
    qk_w, v_w = heads * dqk, heads * dv
    rows = 4 if b % 4 == 0 else 1
    assert v_w == 2 * qk_w and dqk == L and heads <= V7X_LANES and rows <= V7X_SUBLANES
    kernel = functools.partial(_mlstm_kernel, rows=rows, heads=heads, dqk=dqk, dv=dv)
    return pl.pallas_call(
        kernel, grid=(b // rows, s // L),
        in_specs=[
            pl.BlockSpec((rows, L, qk_w), lambda bi, c: (bi, c, 0)),
            pl.BlockSpec((rows, L, qk_w), lambda bi, c: (bi, c, 1)),
            pl.BlockSpec((rows, L, v_w), lambda bi, c: (bi, c, 1)),
            pl.BlockSpec((rows, L, v_w), lambda bi, c: (bi, c, 2)),
            pl.BlockSpec((rows, L, gates.shape[2]), lambda bi, c: (bi, c, 0)),
            pl.BlockSpec((1, gate_bias.shape[1]), lambda bi, c: (0, 0)),
            pl.BlockSpec((1, v_w), lambda bi, c: (0, 0)),
        ],
        out_specs=pl.BlockSpec((rows, L, v_w), lambda bi, c: (bi, c, 0)),
        out_shape=jax.ShapeDtypeStruct((b, s, v_w), BF16),
        scratch_shapes=[pltpu.VMEM((rows * heads, dqk, dv + V7X_LANES), F32),
                        pltpu.VMEM((V7X_SUBLANES, V7X_LANES), F32)],
        compiler_params=_params(("parallel", "arbitrary")),
        name="mlstm",
    )(proj, proj, proj, proj, gates, gate_bias, norm_g)


def _scan_permutation(tt):
    seg = tt // V7X_SUBLANES
    r = jnp.arange(tt)
    src = (r % V7X_SUBLANES) * seg + r // V7X_SUBLANES
    to_interleaved = (src[:, None] == jnp.arange(tt)[None, :]).astype(BF16)
    return to_interleaved, to_interleaved.T


def _sqrt_nonneg(x):
    return jnp.where(x > 0.0, x * lax.rsqrt(x), 0.0)


def _gelu_tanh(x):
    c = (2.0 / jnp.pi) ** 0.5
    hx = 0.5 * x
    return hx + hx * jnp.tanh(x * (c + (c * 0.044715) * (x * x)))


def _rglru_layer_kernel(x_ref, sc_ref, sh_ref, g_ref, win_ref, wout_ref, pm_ref, pmt_ref, cw_ref, cb_ref,
                        wra_ref, bra_ref, wri_ref, bri_ref, lam_ref, o_ref,
                        hp_ref, y_ref, gp_ref, a_ref, u_ref, tail_ref, carry_ref, *, tt, nblk, bw):
    S8 = V7X_SUBLANES
    seg = tt // S8
    H = CONV_W - 1
    d = a_ref.shape[1]

    @pl.when(pl.program_id(1) == 0)
    def _():
        tail_ref[...] = jnp.zeros_like(tail_ref)
        carry_ref[...] = jnp.zeros_like(carry_ref)

    h = _mod_norm(x_ref[0], sc_ref[0], sh_ref[0]).astype(BF16)
    hp_ref[...] = jnp.dot(pm_ref[...], h, preferred_element_type=F32).astype(BF16)

    lam = lam_ref[...]
    half_nsp = (-0.5 * RG_C) * (jnp.maximum(-lam, 0.0) + jnp.log1p(jnp.exp(-jnp.abs(lam))))
    first_segment = lax.broadcasted_iota(jnp.int32, (S8, bw), 0) == 0

    def in_proj(first_col, group):
        cols = slice(first_col + group * GROUP * bw, first_col + (group + 1) * GROUP * bw)
        return jnp.dot(hp_ref[...], win_ref[:, cols], preferred_element_type=F32)

    def project_gate(group):
        gp_ref[:, group * GROUP * bw:(group + 1) * GROUP * bw] = in_proj(d, group)

    GROUP = 2
    early_gate_groups = nblk // GROUP // 2
    ahead = in_proj(0, 0)
    for nb in range(nblk):
        sl = slice(nb * bw, (nb + 1) * bw)
        if nb % GROUP == 0:
            group_out = ahead
            if nb + GROUP < nblk:
                ahead = in_proj(0, nb // GROUP + 1)
            if nb // GROUP < early_gate_groups:
                project_gate(nb // GROUP)
        xp = group_out[:, (nb % GROUP) * bw:(nb % GROUP + 1) * bw]
        hist = []
        for g in range(H):
            cur = xp[(seg - H + g) * S8:(seg - H + g + 1) * S8, :]
            prev = tail_ref[g * S8:(g + 1) * S8, sl]
            hist.append(jnp.where(first_segment, pltpu.roll(prev, 1, 0), pltpu.roll(cur, 1, 0)))
        tail_ref[:, sl] = xp[(seg - H) * S8:, :]
        xext = jnp.concatenate(hist + [xp], axis=0)
        xc = cb_ref[:, sl] + cw_ref[H:H + 1, sl] * xp
        for k in range(H):
            xc = xc + cw_ref[k:k + 1, sl] * xext[k * S8:k * S8 + tt, :]

        xc16 = xc.astype(BF16)
        t_r = jnp.tanh(jnp.dot(xc16, wra_ref[nb], preferred_element_type=F32) + bra_ref[:, sl])
        i = 0.5 * jnp.tanh(jnp.dot(xc16, wri_ref[nb], preferred_element_type=F32) + bri_ref[:, sl]) + 0.5
        log_a = t_r * half_nsp[:, sl] + half_nsp[:, sl]
        a = jnp.exp(log_a)
        one_m_a2 = jnp.tanh(log_a) * (-1.0 - a * a)
        a_ref[:, sl] = a
        u_ref[:, sl] = _sqrt_nonneg(one_m_a2) * (i * xc)

    for group in range(early_gate_groups, nblk // GROUP):
        project_gate(group)
    h_end, p_end = jnp.zeros((S8, d), F32), jnp.ones((S8, d), F32)
    for j in range(seg):
        rows = slice(j * S8, (j + 1) * S8)
        a = a_ref[rows, :]
        h_end = a * h_end + u_ref[rows, :]
        p_end = a * p_end
        u_ref[rows, :] = h_end
        a_ref[rows, :] = p_end

    c = carry_ref[0:1, :]
    carry_rows = []
    for sgi in range(S8):
        carry_rows.append(c)
        c = h_end[sgi:sgi + 1, :] + p_end[sgi:sgi + 1, :] * c
    carry_ref[0:1, :] = c
    carry_in = jnp.concatenate(carry_rows, axis=0)

    pmt = pmt_ref[...]

    def out_proj(group):
        gsl = slice(group * GROUP * bw, (group + 1) * GROUP * bw)
        part = g_ref[0] * jnp.dot(y_ref[:, gsl], wout_ref[gsl, :], preferred_element_type=F32)
        if group == 0:
            o_ref[0] = x_ref[0] + part
        else:
            o_ref[0] += part

    for nb in range(nblk):
        sl = slice(nb * bw, (nb + 1) * bw)
        cin = jnp.concatenate([carry_in[:, sl]] * seg, axis=0)
        hs = u_ref[:, sl] + a_ref[:, sl] * cin
        yi = (hs * _gelu_tanh(gp_ref[:, sl])).astype(BF16)
        y_ref[:, sl] = jnp.dot(pmt, yi, preferred_element_type=F32).astype(BF16)
        if nb % GROUP == GROUP - 1 and nb >= 2 * GROUP - 1:
            out_proj(nb // GROUP - 1)
    out_proj(nblk // GROUP - 1)


def _rglru_layer(x, scale, shift, gate, w_in, w_out, conv_w, conv_b, w_ra, b_ra, w_ri, b_ri, lam, tt=256):
    b, s, dm = x.shape
    d = w_out.shape[0]
    nblk, bw, _ = w_ra.shape
    pm, pmt = _scan_permutation(tt)
    const = lambda shape: pl.BlockSpec(shape, lambda bi, i: (0,) * len(shape))
    resident = lambda shape: pl.BlockSpec(shape, lambda bi, i: (0,) * len(shape),
                                          pipeline_mode=pl.Buffered(1))
    mod_spec = pl.BlockSpec((1, 1, dm), lambda bi, i: (bi, 0, 0))
    kernel = functools.partial(_rglru_layer_kernel, tt=tt, nblk=nblk, bw=bw)
    return pl.pallas_call(
        kernel, grid=(b, s // tt),
        in_specs=[
            pl.BlockSpec((1, tt, dm), lambda bi, i: (bi, i, 0)),
            mod_spec, mod_spec, mod_spec,
            resident((dm, 2 * d)), resident((d, dm)),
            const((tt, tt)), const((tt, tt)),
            const((CONV_W, d)), const((1, d)),
            const((nblk, bw, bw)), const((1, d)), const((nblk, bw, bw)), const((1, d)),
            const((1, d)),
        ],
        out_specs=pl.BlockSpec((1, tt, dm), lambda bi, i: (bi, i, 0)),
        out_shape=jax.ShapeDtypeStruct((b, s, dm), F32),
        scratch_shapes=[pltpu.VMEM((tt, dm), BF16),
                        pltpu.VMEM((tt, d), BF16),
                        pltpu.VMEM((tt, d), F32),
                        pltpu.VMEM((tt, d), F32),
                        pltpu.VMEM((tt, d), F32),
                        pltpu.VMEM(((CONV_W - 1) * V7X_SUBLANES, d), F32),
                        pltpu.VMEM((V7X_SUBLANES, d), F32)],
        compiler_params=_params(("parallel", "arbitrary")),
        name="rglru_layer",
    )(x, scale, shift, gate, w_in, w_out, pm, pmt, conv_w, conv_b, w_ra, b_ra, w_ri, b_ri, lam)


def _gate_lanes(ig, fg):
    pad = [(0, 0)] * (ig.ndim - 1) + [(0, V7X_LANES - ig.shape[-1])]
    return jnp.concatenate([jnp.pad(ig, pad), jnp.pad(fg, pad)], axis=-1)


def kernel(x, c, ada_w, ada_b, a_w_in, a_b_gate, a_norm_g, a_w_out, b_w_in, b_conv_w, b_conv_b,
           b_w_ra, b_b_ra, b_w_ri, b_b_ri, b_lam, b_w_out, mlp_w1, mlp_w2, final_g):
    depth = ada_w.shape[0]
    d = x.shape[-1]
    heads = a_b_gate.shape[-1]
    v_w = a_norm_g.shape[-1]
    qk_w = (a_w_in.shape[-1] - 2 * v_w - 2 * heads) // 2
    dqk, dv = qk_w // heads, v_w // heads
    n_main = 2 * qk_w + 2 * v_w

    mod = _ada_mod(c, ada_w, ada_b)
    final_row = final_g.reshape(1, d)

    def site(layer, j):
        m = mod[layer * 2 + j]
        return (m[:, None, d:2 * d], m[:, None, 0:d], m[:, None, 2 * d:3 * d])

    def big_weights(layer):
        slot = layer // 2
        mixer = ((a_w_in, slot, n_main), (a_w_out, slot, d)) if layer % 2 == 0 else \
                ((b_w_in, slot, b_w_in.shape[-1]), (b_w_out, slot, d))
        return mixer + ((mlp_w1, layer, mlp_w1.shape[-1]), (mlp_w2, layer, d))

    first, lead, _ = big_weights(0)[0]
    w_in16 = first[lead].astype(BF16)
    for layer in range(depth):
        slot = layer // 2
        casts = big_weights(layer)[1:] if layer == 0 else ()
        scale, shift, gate = site(layer, 0)
        if layer % 2 == 0:
            w_g = a_w_in[slot][:, n_main:]
            w_gates = _gate_lanes(w_g[:, :heads], w_g[:, heads:]).astype(BF16)
            proj, gates, *converted = _norm_matmul(x, scale, shift, w_in16, w_gates, casts=casts,
                                                   n_cols=n_main)
            gate_bias = _gate_lanes(a_b_gate[slot, 0:1].astype(F32), a_b_gate[slot, 1:2].astype(F32))
            y = _mlstm(proj, gates, gate_bias, a_norm_g[slot].reshape(1, v_w).astype(F32),
                       heads, dqk, dv)
            if converted:
                w_out16, w1_16, w2_16 = converted
            x = _out_proj_residual(y, w_out16, x, gate)
        else:
            assert not casts
            drnn = b_lam.shape[-1]
            x = _rglru_layer(x, scale, shift, gate, w_in16, w_out16,
                             b_conv_w[slot], b_conv_b[slot].reshape(1, drnn),
                             (0.5 * b_w_ra[slot]).astype(BF16), 0.5 * b_b_ra[slot].reshape(1, drnn),
                             (0.5 * b_w_ri[slot]).astype(BF16), 0.5 * b_b_ri[slot].reshape(1, drnn),
                             b_lam[slot].reshape(1, drnn))
        scale, shift, gate = site(layer, 1)
        last = layer == depth - 1
        x, converted = _mlp(x, scale, shift, gate, w1_16, w2_16, final_row, final_norm=last,
                            casts=() if last else big_weights(layer + 1))
        if converted:
            w_in16, w_out16, w1_16, w2_16 = converted
    return x
```

```python
import functools

import jax
import jax.numpy as jnp
from jax import lax
from jax.experimental import pallas as pl
from jax.experimental.pallas import tpu as pltpu

EPS = 1e-6
RG_C = 8.0
CONV_W = 4
LSTM_CHUNK = 128
V7X_LANES = 128
V7X_SUBLANES = 8
VMEM_LIMIT = 60 * 1024 * 1024

F32 = jnp.float32
BF16 = jnp.bfloat16


def _params(sem):
    return pltpu.CompilerParams(dimension_semantics=sem, vmem_limit_bytes=VMEM_LIMIT)


def _mod_norm(x, scale, shift):
    ms = jnp.mean(x * x, axis=-1, keepdims=True)
    return (x * lax.rsqrt(ms + EPS)) * (1.0 + scale) + shift


BF16_SUBLANES = 16


def _cast_block(r, c, n_steps):
    col_blocks = 1
    while r * col_blocks < n_steps * BF16_SUBLANES:
        col_blocks *= 2
    assert (r * col_blocks) % n_steps == 0 and c % (col_blocks * V7X_LANES) == 0, (r, c, n_steps)
    return (r * col_blocks // n_steps, c // col_blocks), col_blocks


def _cast_specs(casts, n_steps, step_id):
    in_specs, out_specs, out_shapes = [], [], []
    for stacked, lead, cols in casts:
        r = stacked.shape[1]
        block, cb = _cast_block(r, cols, n_steps)
        src_index = lambda *ids, cb=cb, lead=lead: (lead, step_id(*ids) // cb, step_id(*ids) % cb)
        dst_index = lambda *ids, cb=cb: (step_id(*ids) // cb, step_id(*ids) % cb)
        in_specs.append(pl.BlockSpec((None,) + block, src_index))
        out_specs.append(pl.BlockSpec(block, dst_index))
        out_shapes.append(jax.ShapeDtypeStruct((r, cols), BF16))
    return in_specs, out_specs, out_shapes


def _run_casts(src_refs, dst_refs):
    for src, dst in zip(src_refs, dst_refs):
        dst[...] = src[...].astype(dst.dtype)


def _ada_kernel(c_ref, w_ref, b_ref, o_ref):
    c = c_ref[...]
    s = (c * jax.nn.sigmoid(c)).astype(BF16)
    o_ref[0] = jnp.dot(s, w_ref[0].astype(BF16), preferred_element_type=F32) + b_ref[0]


def _ada_mod(c, ada_w, ada_b, tn=1024):
    depth, two, d, n = ada_w.shape
    sites = depth * two
    b = c.shape[0]
    bp = -(-b // V7X_SUBLANES) * V7X_SUBLANES
    c_pad = jnp.pad(c, ((0, bp - b), (0, 0)))
    w = ada_w.reshape(sites, d, n)
    bias = ada_b.reshape(sites, 1, n)
    out = pl.pallas_call(
        _ada_kernel,
        grid=(sites, n // tn),
        in_specs=[
            pl.BlockSpec((bp, d), lambda s, j: (0, 0)),
            pl.BlockSpec((1, d, tn), lambda s, j: (s, 0, j)),
            pl.BlockSpec((1, 1, tn), lambda s, j: (s, 0, j)),
        ],
        out_specs=pl.BlockSpec((1, bp, tn), lambda s, j: (s, 0, j)),
        out_shape=jax.ShapeDtypeStruct((sites, bp, n), F32),
        compiler_params=_params(("parallel", "parallel")),
        name="ada_mod",
    )(c_pad, w, bias)
    return out[:, :b, :]


def _norm_matmul_kernel(*refs, has_gates, n_cast, cast_j):
    n_in = 4 + has_gates
    x_ref, sc_ref, sh_ref, w_ref = refs[:4]
    cast_src = refs[n_in:n_in + n_cast]
    outs = refs[n_in + n_cast:-1]
    o_ref, cast_dst, h_ref = outs[0], outs[1 + has_gates:], refs[-1]
    j = pl.program_id(2)

    def project():
        o_ref[0] = jnp.dot(h_ref[...], w_ref[...], preferred_element_type=F32).astype(o_ref.dtype)

    @pl.when(j == 0)
    def _():
        h_ref[...] = _mod_norm(x_ref[0], sc_ref[0], sh_ref[0]).astype(BF16)
        if has_gates:
            outs[1][0] = jnp.dot(h_ref[...], refs[4][...], preferred_element_type=F32)
        project()

    pl.when(j > 0)(project)

    if n_cast:
        pl.when(j < cast_j)(lambda: _run_casts(cast_src, cast_dst))


def _norm_matmul(x, scale, shift, w, w_gates=None, casts=(), n_cols=None, tm=1024, tn=1024):
    b, s, d = x.shape
    n = w.shape[1] if n_cols is None else n_cols
    nj = n // tn
    grid = (b, s // tm, nj)
    cast_j = 1 << (nj.bit_length() - 1)
    step_id = lambda bi, i, j: (bi * (s // tm) + i) * cast_j + jnp.minimum(j, cast_j - 1)
    c_in, c_out, c_shapes = _cast_specs(casts, b * (s // tm) * cast_j, step_id)
    mod_spec = pl.BlockSpec((1, 1, d), lambda bi, i, j: (bi, 0, 0))
    in_specs = [pl.BlockSpec((1, tm, d), lambda bi, i, j: (bi, i, 0)), mod_spec, mod_spec,
                pl.BlockSpec((d, tn), lambda bi, i, j: (0, j))]
    out_specs = [pl.BlockSpec((1, tm, tn), lambda bi, i, j: (bi, i, j))]
    out_shapes = [jax.ShapeDtypeStruct((b, s, n), BF16)]
    args = [x, scale, shift, w]
    if w_gates is not None:
        ng = w_gates.shape[1]
        in_specs.append(pl.BlockSpec((d, ng), lambda bi, i, j: (0, 0)))
        out_specs.append(pl.BlockSpec((1, tm, ng), lambda bi, i, j: (bi, i, 0)))
        out_shapes.append(jax.ShapeDtypeStruct((b, s, ng), F32))
        args.append(w_gates)
    kernel = functools.partial(_norm_matmul_kernel, has_gates=w_gates is not None,
                               n_cast=len(casts), cast_j=cast_j)
    return pl.pallas_call(
        kernel, grid=grid,
        in_specs=in_specs + c_in, out_specs=out_specs + c_out, out_shape=out_shapes + c_shapes,
        scratch_shapes=[pltpu.VMEM((tm, d), BF16)],
        compiler_params=_params(("parallel", "parallel", "arbitrary")),
        name="norm_matmul_gates" if w_gates is not None else "norm_matmul",
    )(*args, *[stacked for stacked, _, _ in casts])


def _out_proj_kernel(y_ref, w_ref, x_ref, g_ref, o_ref):
    acc = jnp.dot(y_ref[0], w_ref[...], preferred_element_type=F32)
    o_ref[0] = x_ref[0] + g_ref[0] * acc


def _out_proj_residual(y, w, x, gate, tm=512, tn=2048):
    b, s, k = y.shape
    n = w.shape[1]
    return pl.pallas_call(
        _out_proj_kernel, grid=(b, s // tm, n // tn),
        in_specs=[
            pl.BlockSpec((1, tm, k), lambda bi, i, j: (bi, i, 0)),
            pl.BlockSpec((k, tn), lambda bi, i, j: (0, j)),
            pl.BlockSpec((1, tm, tn), lambda bi, i, j: (bi, i, j)),
            pl.BlockSpec((1, 1, tn), lambda bi, i, j: (bi, 0, j)),
        ],
        out_specs=pl.BlockSpec((1, tm, tn), lambda bi, i, j: (bi, i, j)),
        out_shape=jax.ShapeDtypeStruct((b, s, n), F32),
        compiler_params=_params(("parallel", "parallel", "arbitrary")),
        name="out_proj_residual",
    )(y, w, x, gate)


def _mlp_kernel(*refs, final_norm, n_cast):
    x_ref, sc_ref, sh_ref, g_ref, w1_ref, w2_ref, fg_ref = refs[:7]
    cast_src = refs[7:7 + n_cast]
    o_ref, cast_dst, h_ref = refs[7 + n_cast], refs[8 + n_cast:-1], refs[-1]
    f = pl.program_id(2)

    def ffn_slice():
        u = jnp.dot(h_ref[...], w1_ref[...], preferred_element_type=F32)
        u = jnp.square(jnp.maximum(u, 0.0)).astype(BF16)
        return g_ref[0] * jnp.dot(u, w2_ref[...], preferred_element_type=F32)

    @pl.when(f == 0)
    def _():
        h_ref[...] = _mod_norm(x_ref[0], sc_ref[0], sh_ref[0]).astype(BF16)
        o_ref[0] = x_ref[0] + ffn_slice()

    @pl.when(f > 0)
    def _():
        o_ref[0] += ffn_slice()

    _run_casts(cast_src, cast_dst)

    if final_norm:
        @pl.when(f == pl.num_programs(2) - 1)
        def _():
            y = o_ref[0]
            ms = jnp.mean(y * y, axis=-1, keepdims=True)
            o_ref[0] = (y * lax.rsqrt(ms + EPS)) * fg_ref[...]


def _mlp(x, scale, shift, gate, w1, w2, final_g, final_norm, casts=(), tm=1024, tf=None):
    b, s, d = x.shape
    dff = w1.shape[1]
    if tf is None:
        tf = 512 if casts else 1024
    ni, nf = s // tm, dff // tf
    c_in, c_out, c_shapes = _cast_specs(casts, b * ni * nf, lambda bi, i, f: (bi * ni + i) * nf + f)
    mod_spec = pl.BlockSpec((1, 1, d), lambda bi, i, f: (bi, 0, 0))
    out = pl.pallas_call(
        functools.partial(_mlp_kernel, final_norm=final_norm, n_cast=len(casts)),
        grid=(b, ni, nf),
        in_specs=[
            pl.BlockSpec((1, tm, d), lambda bi, i, f: (bi, i, 0)),
            mod_spec, mod_spec, mod_spec,
            pl.BlockSpec((d, tf), lambda bi, i, f: (0, f)),
            pl.BlockSpec((tf, d), lambda bi, i, f: (f, 0)),
            pl.BlockSpec((1, d), lambda bi, i, f: (0, 0)),
        ] + c_in,
        out_specs=[pl.BlockSpec((1, tm, d), lambda bi, i, f: (bi, i, 0))] + c_out,
        out_shape=[jax.ShapeDtypeStruct((b, s, d), F32)] + c_shapes,
        scratch_shapes=[pltpu.VMEM((tm, d), BF16)],
        compiler_params=_params(("parallel", "parallel", "arbitrary")),
        name="mlp_final" if final_norm else "mlp",
    )(x, scale, shift, gate, w1, w2, final_g, *[stacked for stacked, _, _ in casts])
    return out[0], out[1:]


def _cumsum_rows(x, tril):
    x1 = x.astype(BF16)
    r1 = x - x1.astype(F32)
    x2 = r1.astype(BF16)
    x3 = (r1 - x2.astype(F32)).astype(BF16)
    d = lambda t: jnp.dot(tril, t, preferred_element_type=F32)
    return d(x1) + d(x2) + d(x3)


def _cummax_rows(x):
    n = x.shape[0]
    row = lax.broadcasted_iota(jnp.int32, x.shape, 0)
    sh = 1
    while sh < n:
        if sh < V7X_SUBLANES:
            prev = jnp.where(row >= sh, pltpu.roll(x, sh, 0), -jnp.inf)
        else:
            prev = jnp.concatenate([jnp.full((sh, x.shape[1]), -jnp.inf, x.dtype), x[:n - sh]], axis=0)
        x = jnp.maximum(x, prev)
        sh *= 2
    return x


def _log_sigmoid(z):
    return jnp.minimum(z, 0.0) - jnp.log1p(jnp.exp(-jnp.abs(z)))


def _mlstm_kernel(q_ref, k_ref, v_ref, o_ref, g_ref, bias_ref, ng_ref, y_ref, cn_ref, m_ref,
                  *, rows, heads, dqk, dv):
    L, LN = LSTM_CHUNK, V7X_LANES

    @pl.when(pl.program_id(1) == 0)
    def _():
        cn_ref[...] = jnp.zeros_like(cn_ref)
        m_ref[...] = jnp.zeros_like(m_ref)

    row = lax.broadcasted_iota(jnp.int32, (L, L), 0)
    col = lax.broadcasted_iota(jnp.int32, (L, L), 1)
    causal = row >= col
    tril = jnp.where(causal, 1.0, 0.0).astype(BF16)
    eye = jnp.where(lax.broadcasted_iota(jnp.int32, (dqk, dqk), 0)
                    == lax.broadcasted_iota(jnp.int32, (dqk, dqk), 1), 1.0, 0.0).astype(BF16)

    b, cm, m_row, decay_row, r_rows, wk_rows = [], [], [], [], [], []
    for bb in range(rows):
        log_in = g_ref[bb, :, 0:LN] + bias_ref[:, 0:LN]
        b.append(_cumsum_rows(_log_sigmoid(g_ref[bb, :, LN:2 * LN] + bias_ref[:, LN:2 * LN]), tril))
        r = log_in - b[bb]
        cm.append(_cummax_rows(r))
        m_row.append(m_ref[bb:bb + 1, :])
        mx_last = jnp.maximum(m_row[bb], cm[bb][L - 1:L, :])
        decay_row.append(jnp.exp(m_row[bb] - mx_last))
        wk = jnp.exp(r - mx_last)
        m_ref[bb:bb + 1, :] = b[bb][L - 1:L, :] + mx_last
        r_rows.append(r.T)
        wk_rows.append(wk.T)

    floor_scale = float(dqk) ** 0.5
    ones_blk = jnp.ones((L, LN), BF16)
    nt = (((1,), (1,)), ((), ()))
    units = [(bb, h) for bb in range(rows) for h in range(heads)]
    ur = range(len(units))
    q = [q_ref[bb, :, h * dqk:(h + 1) * dqk] for bb, h in units]
    k = [k_ref[bb, :, h * dqk:(h + 1) * dqk] for bb, h in units]
    s = [lax.dot_general(q[u], k[u], nt, preferred_element_type=F32) for u in ur]
    k_t = [lax.dot_general(eye, k[u], nt, preferred_element_type=F32) for u in ur]

    lhs, rhs, vo, floor = [], [], [], []
    for u, (bb, h) in enumerate(units):
        m_h = m_row[bb][:, h:h + 1]
        mx = jnp.maximum(jnp.broadcast_to(cm[bb][:, h:h + 1], (L, LN)), m_h)
        b_t = jnp.broadcast_to(b[bb][:, h:h + 1], (L, LN))
        p = jnp.exp(jnp.where(causal, r_rows[bb][h:h + 1, :] - mx, -jnp.inf)) * s[u]
        wq = (q[u].astype(F32) * jnp.exp(m_h - mx)).astype(BF16)
        lhs.append(jnp.concatenate([wq, p.astype(BF16)], axis=1))
        vo.append(jnp.concatenate([v_ref[bb, :, h * dv:(h + 1) * dv], ones_blk], axis=1))
        rhs.append(jnp.concatenate([cn_ref[u].astype(BF16), vo[u]], axis=0))
        floor.append(jnp.exp(-(b_t + mx)) * floor_scale)

    out = [jnp.dot(lhs[u], rhs[u], preferred_element_type=F32) for u in ur]

    for u, (bb, h) in enumerate(units):
        kw_t = (k_t[u] * wk_rows[bb][h:h + 1, :]).astype(BF16)
        cn_ref[u] = (decay_row[bb][:, h:h + 1] * cn_ref[u]
                     + jnp.dot(kw_t, vo[u], preferred_element_type=F32))

    for u, (bb, h) in enumerate(units):
        inv = 1.0 / jnp.maximum(jnp.abs(out[u][:, dv:]), floor[u])
        hh = out[u][:, :dv] * jnp.concatenate([inv] * (dv // LN), axis=1)
        ms = jnp.mean(hh * hh, axis=1, keepdims=True)
        hn = (hh * lax.rsqrt(ms + EPS)) * ng_ref[:, h * dv:(h + 1) * dv]
        og = 0.5 * jnp.tanh(0.5 * o_ref[bb, :, h * dv:(h + 1) * dv].astype(F32)) + 0.5
        y_ref[bb, :, h * dv:(h + 1) * dv] = (hn * og).astype(y_ref.dtype)


def _mlstm(proj, gates, gate_bias, norm_g, heads, dqk, dv):
    b, s, _ = proj.shape
    L = LSTM_CHUNK
    qk_w, v_w = heads * dqk, heads * dv
    rows = 4 if b % 4 == 0 else 1
    assert v_w == 2 * qk_w and dqk == L and heads <= V7X_LANES and rows <= V7X_SUBLANES
    kernel = functools.partial(_mlstm_kernel, rows=rows, heads=heads, dqk=dqk, dv=dv)
    return pl.pallas_call(
        kernel, grid=(b // rows, s // L),
        in_specs=[
            pl.BlockSpec((rows, L, qk_w), lambda bi, c: (bi, c, 0)),
            pl.BlockSpec((rows, L, qk_w), lambda bi, c: (bi, c, 1)),
            pl.BlockSpec((rows, L, v_w), lambda bi, c: (bi, c, 1)),
            pl.BlockSpec((rows, L, v_w), lambda bi, c: (bi, c, 2)),
            pl.BlockSpec((rows, L, gates.shape[2]), lambda bi, c: (bi, c, 0)),
            pl.BlockSpec((1, gate_bias.shape[1]), lambda bi, c: (0, 0)),
            pl.BlockSpec((1, v_w), lambda bi, c: (0, 0)),
        ],
        out_specs=pl.BlockSpec((rows, L, v_w), lambda bi, c: (bi, c, 0)),
        out_shape=jax.ShapeDtypeStruct((b, s, v_w), BF16),
        scratch_shapes=[pltpu.VMEM((rows * heads, dqk, dv + V7X_LANES), F32),
                        pltpu.VMEM((V7X_SUBLANES, V7X_LANES), F32)],
        compiler_params=_params(("parallel", "arbitrary")),
        name="mlstm",
    )(proj, proj, proj, proj, gates, gate_bias, norm_g)


def _scan_permutation(tt):
    seg = tt // V7X_SUBLANES
    r = jnp.arange(tt)
    src = (r % V7X_SUBLANES) * seg + r // V7X_SUBLANES
    to_interleaved = (src[:, None] == jnp.arange(tt)[None, :]).astype(BF16)
    return to_interleaved, to_interleaved.T


def _sqrt_nonneg(x):
    return jnp.where(x > 0.0, x * lax.rsqrt(x), 0.0)


def _gelu_tanh(x):
    c = (2.0 / jnp.pi) ** 0.5
    hx = 0.5 * x
    return hx + hx * jnp.tanh(x * (c + (c * 0.044715) * (x * x)))


def _rglru_layer_kernel(x_ref, sc_ref, sh_ref, g_ref, win_ref, wout_ref, pm_ref, pmt_ref, cw_ref, cb_ref,
                        wra_ref, bra_ref, wri_ref, bri_ref, lam_ref, o_ref,
                        hp_ref, y_ref, gp_ref, a_ref, u_ref, tail_ref, carry_ref, *, tt, nblk, bw):
    S8 = V7X_SUBLANES
    seg = tt // S8
    H = CONV_W - 1
    d = a_ref.shape[1]

    @pl.when(pl.program_id(1) == 0)
    def _():
        tail_ref[...] = jnp.zeros_like(tail_ref)
        carry_ref[...] = jnp.zeros_like(carry_ref)

    h = _mod_norm(x_ref[0], sc_ref[0], sh_ref[0]).astype(BF16)
    hp_ref[...] = jnp.dot(pm_ref[...], h, preferred_element_type=F32).astype(BF16)

    lam = lam_ref[...]
    half_nsp = (-0.5 * RG_C) * (jnp.maximum(-lam, 0.0) + jnp.log1p(jnp.exp(-jnp.abs(lam))))
    first_segment = lax.broadcasted_iota(jnp.int32, (S8, bw), 0) == 0

    def in_proj(first_col, group):
        cols = slice(first_col + group * GROUP * bw, first_col + (group + 1) * GROUP * bw)
        return jnp.dot(hp_ref[...], win_ref[:, cols], preferred_element_type=F32)

    def project_gate(group):
        gp_ref[:, group * GROUP * bw:(group + 1) * GROUP * bw] = in_proj(d, group)

    GROUP = 2
    early_gate_groups = nblk // GROUP // 2
    ahead = in_proj(0, 0)
    for nb in range(nblk):
        sl = slice(nb * bw, (nb + 1) * bw)
        if nb % GROUP == 0:
            group_out = ahead
            if nb + GROUP < nblk:
                ahead = in_proj(0, nb // GROUP + 1)
            if nb // GROUP < early_gate_groups:
                project_gate(nb // GROUP)
        xp = group_out[:, (nb % GROUP) * bw:(nb % GROUP + 1) * bw]
        hist = []
        for g in range(H):
            cur = xp[(seg - H + g) * S8:(seg - H + g + 1) * S8, :]
            prev = tail_ref[g * S8:(g + 1) * S8, sl]
            hist.append(jnp.where(first_segment, pltpu.roll(prev, 1, 0), pltpu.roll(cur, 1, 0)))
        tail_ref[:, sl] = xp[(seg - H) * S8:, :]
        xext = jnp.concatenate(hist + [xp], axis=0)
        xc = cb_ref[:, sl] + cw_ref[H:H + 1, sl] * xp
        for k in range(H):
            xc = xc + cw_ref[k:k + 1, sl] * xext[k * S8:k * S8 + tt, :]

        xc16 = xc.astype(BF16)
        t_r = jnp.tanh(jnp.dot(xc16, wra_ref[nb], preferred_element_type=F32) + bra_ref[:, sl])
        i = 0.5 * jnp.tanh(jnp.dot(xc16, wri_ref[nb], preferred_element_type=F32) + bri_ref[:, sl]) + 0.5
        log_a = t_r * half_nsp[:, sl] + half_nsp[:, sl]
        a = jnp.exp(log_a)
        one_m_a2 = jnp.tanh(log_a) * (-1.0 - a * a)
        a_ref[:, sl] = a
        u_ref[:, sl] = _sqrt_nonneg(one_m_a2) * (i * xc)

    for group in range(early_gate_groups, nblk // GROUP):
        project_gate(group)
    h_end, p_end = jnp.zeros((S8, d), F32), jnp.ones((S8, d), F32)
    for j in range(seg):
        rows = slice(j * S8, (j + 1) * S8)
        a = a_ref[rows, :]
        h_end = a * h_end + u_ref[rows, :]
        p_end = a * p_end
        u_ref[rows, :] = h_end
        a_ref[rows, :] = p_end

    c = carry_ref[0:1, :]
    carry_rows = []
    for sgi in range(S8):
        carry_rows.append(c)
        c = h_end[sgi:sgi + 1, :] + p_end[sgi:sgi + 1, :] * c
    carry_ref[0:1, :] = c
    carry_in = jnp.concatenate(carry_rows, axis=0)

    pmt = pmt_ref[...]

    def out_proj(group):
        gsl = slice(group * GROUP * bw, (group + 1) * GROUP * bw)
        part = g_ref[0] * jnp.dot(y_ref[:, gsl], wout_ref[gsl, :], preferred_element_type=F32)
        if group == 0:
            o_ref[0] = x_ref[0] + part
        else:
            o_ref[0] += part

    for nb in range(nblk):
        sl = slice(nb * bw, (nb + 1) * bw)
        cin = jnp.concatenate([carry_in[:, sl]] * seg, axis=0)
        hs = u_ref[:, sl] + a_ref[:, sl] * cin
        yi = (hs * _gelu_tanh(gp_ref[:, sl])).astype(BF16)
        y_ref[:, sl] = jnp.dot(pmt, yi, preferred_element_type=F32).astype(BF16)
        if nb % GROUP == GROUP - 1 and nb >= 2 * GROUP - 1:
            out_proj(nb // GROUP - 1)
    out_proj(nblk // GROUP - 1)


def _rglru_layer(x, scale, shift, gate, w_in, w_out, conv_w, conv_b, w_ra, b_ra, w_ri, b_ri, lam, tt=256):
    b, s, dm = x.shape
    d = w_out.shape[0]
    nblk, bw, _ = w_ra.shape
    pm, pmt = _scan_permutation(tt)
    const = lambda shape: pl.BlockSpec(shape, lambda bi, i: (0,) * len(shape))
    resident = lambda shape: pl.BlockSpec(shape, lambda bi, i: (0,) * len(shape),
                                          pipeline_mode=pl.Buffered(1))
    mod_spec = pl.BlockSpec((1, 1, dm), lambda bi, i: (bi, 0, 0))
    kernel = functools.partial(_rglru_layer_kernel, tt=tt, nblk=nblk, bw=bw)
    return pl.pallas_call(
        kernel, grid=(b, s // tt),
        in_specs=[
            pl.BlockSpec((1, tt, dm), lambda bi, i: (bi, i, 0)),
            mod_spec, mod_spec, mod_spec,
            resident((dm, 2 * d)), resident((d, dm)),
            const((tt, tt)), const((tt, tt)),
            const((CONV_W, d)), const((1, d)),
            const((nblk, bw, bw)), const((1, d)), const((nblk, bw, bw)), const((1, d)),
            const((1, d)),
        ],
        out_specs=pl.BlockSpec((1, tt, dm), lambda bi, i: (bi, i, 0)),
        out_shape=jax.ShapeDtypeStruct((b, s, dm), F32),
        scratch_shapes=[pltpu.VMEM((tt, dm), BF16),
                        pltpu.VMEM((tt, d), BF16),
                        pltpu.VMEM((tt, d), F32),
                        pltpu.VMEM((tt, d), F32),
                        pltpu.VMEM((tt, d), F32),
                        pltpu.VMEM(((CONV_W - 1) * V7X_SUBLANES, d), F32),
                        pltpu.VMEM((V7X_SUBLANES, d), F32)],
        compiler_params=_params(("parallel", "arbitrary")),
        name="rglru_layer",
    )(x, scale, shift, gate, w_in, w_out, pm, pmt, conv_w, conv_b, w_ra, b_ra, w_ri, b_ri, lam)


def _gate_lanes(ig, fg):
    pad = [(0, 0)] * (ig.ndim - 1) + [(0, V7X_LANES - ig.shape[-1])]
    return jnp.concatenate([jnp.pad(ig, pad), jnp.pad(fg, pad)], axis=-1)


def kernel(x, c, ada_w, ada_b, a_w_in, a_b_gate, a_norm_g, a_w_out, b_w_in, b_conv_w, b_conv_b,
           b_w_ra, b_b_ra, b_w_ri, b_b_ri, b_lam, b_w_out, mlp_w1, mlp_w2, final_g):
    depth = ada_w.shape[0]
    d = x.shape[-1]
    heads = a_b_gate.shape[-1]
    v_w = a_norm_g.shape[-1]
    qk_w = (a_w_in.shape[-1] - 2 * v_w - 2 * heads) // 2
    dqk, dv = qk_w // heads, v_w // heads
    n_main = 2 * qk_w + 2 * v_w

    mod = _ada_mod(c, ada_w, ada_b)
    final_row = final_g.reshape(1, d)

    def site(layer, j):
        m = mod[layer * 2 + j]
        return (m[:, None, d:2 * d], m[:, None, 0:d], m[:, None, 2 * d:3 * d])

    def big_weights(layer):
        slot = layer // 2
        mixer = ((a_w_in, slot, n_main), (a_w_out, slot, d)) if layer % 2 == 0 else \
                ((b_w_in, slot, b_w_in.shape[-1]), (b_w_out, slot, d))
        return mixer + ((mlp_w1, layer, mlp_w1.shape[-1]), (mlp_w2, layer, d))

    first, lead, _ = big_weights(0)[0]
    w_in16 = first[lead].astype(BF16)
    next16 = None
    for layer in range(depth):
        slot = layer // 2
        casts = ()
        if layer == 0:
            casts = big_weights(0)[1:] + (big_weights(1) if depth > 1 else ())
        elif next16 is not None:
            w_in16, w_out16, w1_16, w2_16 = next16
        scale, shift, gate = site(layer, 0)
        if layer % 2 == 0:
            w_g = a_w_in[slot][:, n_main:]
            w_gates = _gate_lanes(w_g[:, :heads], w_g[:, heads:]).astype(BF16)
            proj, gates, *converted = _norm_matmul(x, scale, shift, w_in16, w_gates, casts=casts,
                                                   n_cols=n_main)
            gate_bias = _gate_lanes(a_b_gate[slot, 0:1].astype(F32), a_b_gate[slot, 1:2].astype(F32))
            y = _mlstm(proj, gates, gate_bias, a_norm_g[slot].reshape(1, v_w).astype(F32),
                       heads, dqk, dv)
            if converted:
                w_out16, w1_16, w2_16 = converted[:3]
                next16 = converted[3:] or None
            x = _out_proj_residual(y, w_out16, x, gate)
        else:
            assert not casts
            drnn = b_lam.shape[-1]
            x = _rglru_layer(x, scale, shift, gate, w_in16, w_out16,
                             b_conv_w[slot], b_conv_b[slot].reshape(1, drnn),
                             (0.5 * b_w_ra[slot]).astype(BF16), 0.5 * b_b_ra[slot].reshape(1, drnn),
                             (0.5 * b_w_ri[slot]).astype(BF16), 0.5 * b_b_ri[slot].reshape(1, drnn),
                             b_lam[slot].reshape(1, drnn))
        scale, shift, gate = site(layer, 1)
        last = layer == depth - 1
        host = 1 <= layer < depth - 1
        x, converted = _mlp(x, scale, shift, gate, w1_16, w2_16, final_row, final_norm=last,
                            casts=big_weights(layer + 1) if host else ())
        if converted:
            next16 = converted
    return x
```

```python
import functools

import jax
import jax.numpy as jnp
from jax import lax
from jax.experimental import pallas as pl
from jax.experimental.pallas import tpu as pltpu

EPS = 1e-6
RG_C = 8.0
CONV_W = 4
LSTM_CHUNK = 128
V7X_LANES = 128
V7X_SUBLANES = 8
VMEM_LIMIT = 60 * 1024 * 1024

F32 = jnp.float32
BF16 = jnp.bfloat16


def _params(sem):
    return pltpu.CompilerParams(dimension_semantics=sem, vmem_limit_bytes=VMEM_LIMIT)


def _mod_norm(x, scale, shift):
    ms = jnp.mean(x * x, axis=-1, keepdims=True)
    return (x * lax.rsqrt(ms + EPS)) * (1.0 + scale) + shift


BF16_SUBLANES = 16


def _cast_block(r, c, n_steps):
    col_blocks = 1
    while r * col_blocks < n_steps * BF16_SUBLANES:
        col_blocks *= 2
    assert (r * col_blocks) % n_steps == 0 and c % (col_blocks * V7X_LANES) == 0, (r, c, n_steps)
    return (r * col_blocks // n_steps, c // col_blocks), col_blocks


def _cast_specs(casts, n_steps, step_id):
    in_specs, out_specs, out_shapes = [], [], []
    for stacked, lead, cols in casts:
        r = stacked.shape[1]
        block, cb = _cast_block(r, cols, n_steps)
        src_index = lambda *ids, cb=cb, lead=lead: (lead, step_id(*ids) // cb, step_id(*ids) % cb)
        dst_index = lambda *ids, cb=cb: (step_id(*ids) // cb, step_id(*ids) % cb)
        in_specs.append(pl.BlockSpec((None,) + block, src_index))
        out_specs.append(pl.BlockSpec(block, dst_index))
        out_shapes.append(jax.ShapeDtypeStruct((r, cols), BF16))
    return in_specs, out_specs, out_shapes


def _run_casts(src_refs, dst_refs):
    for src, dst in zip(src_refs, dst_refs):
        dst[...] = src[...].astype(dst.dtype)


def _ada_kernel(c_ref, w_ref, b_ref, o_ref):
    c = c_ref[...]
    s = (c * jax.nn.sigmoid(c)).astype(BF16)
    o_ref[0] = jnp.dot(s, w_ref[0].astype(BF16), preferred_element_type=F32) + b_ref[0]


def _ada_mod(c, ada_w, ada_b, tn=1024):
    depth, two, d, n = ada_w.shape
    sites = depth * two
    b = c.shape[0]
    bp = -(-b // V7X_SUBLANES) * V7X_SUBLANES
    c_pad = jnp.pad(c, ((0, bp - b), (0, 0)))
    w = ada_w.reshape(sites, d, n)
    bias = ada_b.reshape(sites, 1, n)
    out = pl.pallas_call(
        _ada_kernel,
        grid=(sites, n // tn),
        in_specs=[
            pl.BlockSpec((bp, d), lambda s, j: (0, 0)),
            pl.BlockSpec((1, d, tn), lambda s, j: (s, 0, j)),
            pl.BlockSpec((1, 1, tn), lambda s, j: (s, 0, j)),
        ],
        out_specs=pl.BlockSpec((1, bp, tn), lambda s, j: (s, 0, j)),
        out_shape=jax.ShapeDtypeStruct((sites, bp, n), F32),
        compiler_params=_params(("parallel", "parallel")),
        name="ada_mod",
    )(c_pad, w, bias)
    return out[:, :b, :]


def _norm_matmul_kernel(*refs, has_gates, n_cast, cast_j):
    n_in = 4 + has_gates
    x_ref, sc_ref, sh_ref, w_ref = refs[:4]
    cast_src = refs[n_in:n_in + n_cast]
    outs = refs[n_in + n_cast:-1]
    o_ref, cast_dst, h_ref = outs[0], outs[1 + has_gates:], refs[-1]
    j = pl.program_id(2)

    def project():
        o_ref[0] = jnp.dot(h_ref[...], w_ref[...], preferred_element_type=F32).astype(o_ref.dtype)

    @pl.when(j == 0)
    def _():
        h_ref[...] = _mod_norm(x_ref[0], sc_ref[0], sh_ref[0]).astype(BF16)
        if has_gates:
            outs[1][0] = jnp.dot(h_ref[...], refs[4][...], preferred_element_type=F32)
        project()

    pl.when(j > 0)(project)

    if n_cast:
        pl.when(j < cast_j)(lambda: _run_casts(cast_src, cast_dst))


def _norm_matmul(x, scale, shift, w, w_gates=None, casts=(), n_cols=None, tm=1024, tn=1024):
    b, s, d = x.shape
    n = w.shape[1] if n_cols is None else n_cols
    nj = n // tn
    grid = (b, s // tm, nj)
    cast_j = 1 << (nj.bit_length() - 1)
    step_id = lambda bi, i, j: (bi * (s // tm) + i) * cast_j + jnp.minimum(j, cast_j - 1)
    c_in, c_out, c_shapes = _cast_specs(casts, b * (s // tm) * cast_j, step_id)
    mod_spec = pl.BlockSpec((1, 1, d), lambda bi, i, j: (bi, 0, 0))
    in_specs = [pl.BlockSpec((1, tm, d), lambda bi, i, j: (bi, i, 0)), mod_spec, mod_spec,
                pl.BlockSpec((d, tn), lambda bi, i, j: (0, j))]
    out_specs = [pl.BlockSpec((1, tm, tn), lambda bi, i, j: (bi, i, j))]
    out_shapes = [jax.ShapeDtypeStruct((b, s, n), BF16)]
    args = [x, scale, shift, w]
    if w_gates is not None:
        ng = w_gates.shape[1]
        in_specs.append(pl.BlockSpec((d, ng), lambda bi, i, j: (0, 0)))
        out_specs.append(pl.BlockSpec((1, tm, ng), lambda bi, i, j: (bi, i, 0)))
        out_shapes.append(jax.ShapeDtypeStruct((b, s, ng), F32))
        args.append(w_gates)
    kernel = functools.partial(_norm_matmul_kernel, has_gates=w_gates is not None,
                               n_cast=len(casts), cast_j=cast_j)
    return pl.pallas_call(
        kernel, grid=grid,
        in_specs=in_specs + c_in, out_specs=out_specs + c_out, out_shape=out_shapes + c_shapes,
        scratch_shapes=[pltpu.VMEM((tm, d), BF16)],
        compiler_params=_params(("parallel", "parallel", "arbitrary")),
        name="norm_matmul_gates" if w_gates is not None else "norm_matmul",
    )(*args, *[stacked for stacked, _, _ in casts])


def _out_proj_kernel(y_ref, w_ref, x_ref, g_ref, o_ref):
    acc = jnp.dot(y_ref[0], w_ref[...], preferred_element_type=F32)
    o_ref[0] = x_ref[0] + g_ref[0] * acc


def _out_proj_residual(y, w, x, gate, tm=512, tn=2048):
    b, s, k = y.shape
    n = w.shape[1]
    return pl.pallas_call(
        _out_proj_kernel, grid=(b, s // tm, n // tn),
        in_specs=[
            pl.BlockSpec((1, tm, k), lambda bi, i, j: (bi, i, 0)),
            pl.BlockSpec((k, tn), lambda bi, i, j: (0, j)),
            pl.BlockSpec((1, tm, tn), lambda bi, i, j: (bi, i, j)),
            pl.BlockSpec((1, 1, tn), lambda bi, i, j: (bi, 0, j)),
        ],
        out_specs=pl.BlockSpec((1, tm, tn), lambda bi, i, j: (bi, i, j)),
        out_shape=jax.ShapeDtypeStruct((b, s, n), F32),
        compiler_params=_params(("parallel", "parallel", "arbitrary")),
        name="out_proj_residual",
    )(y, w, x, gate)


def _mlp_kernel(*refs, final_norm, n_cast):
    x_ref, sc_ref, sh_ref, g_ref, w1_ref, w2_ref, fg_ref = refs[:7]
    cast_src = refs[7:7 + n_cast]
    o_ref, cast_dst, h_ref = refs[7 + n_cast], refs[8 + n_cast:-1], refs[-1]
    f = pl.program_id(2)

    def ffn_slice():
        u = jnp.dot(h_ref[...], w1_ref[...], preferred_element_type=F32)
        u = jnp.square(jnp.maximum(u, 0.0)).astype(BF16)
        return g_ref[0] * jnp.dot(u, w2_ref[...], preferred_element_type=F32)

    @pl.when(f == 0)
    def _():
        h_ref[...] = _mod_norm(x_ref[0], sc_ref[0], sh_ref[0]).astype(BF16)
        o_ref[0] = x_ref[0] + ffn_slice()

    @pl.when(f > 0)
    def _():
        o_ref[0] += ffn_slice()

    _run_casts(cast_src, cast_dst)

    if final_norm:
        @pl.when(f == pl.num_programs(2) - 1)
        def _():
            y = o_ref[0]
            ms = jnp.mean(y * y, axis=-1, keepdims=True)
            o_ref[0] = (y * lax.rsqrt(ms + EPS)) * fg_ref[...]


def _mlp(x, scale, shift, gate, w1, w2, final_g, final_norm, casts=(), tm=1024, tf=None):
    b, s, d = x.shape
    dff = w1.shape[1]
    if tf is None:
        tf = 512 if casts else 1024
    ni, nf = s // tm, dff // tf
    c_in, c_out, c_shapes = _cast_specs(casts, b * ni * nf, lambda bi, i, f: (bi * ni + i) * nf + f)
    mod_spec = pl.BlockSpec((1, 1, d), lambda bi, i, f: (bi, 0, 0))
    out = pl.pallas_call(
        functools.partial(_mlp_kernel, final_norm=final_norm, n_cast=len(casts)),
        grid=(b, ni, nf),
        in_specs=[
            pl.BlockSpec((1, tm, d), lambda bi, i, f: (bi, i, 0)),
            mod_spec, mod_spec, mod_spec,
            pl.BlockSpec((d, tf), lambda bi, i, f: (0, f)),
            pl.BlockSpec((tf, d), lambda bi, i, f: (f, 0)),
            pl.BlockSpec((1, d), lambda bi, i, f: (0, 0)),
        ] + c_in,
        out_specs=[pl.BlockSpec((1, tm, d), lambda bi, i, f: (bi, i, 0))] + c_out,
        out_shape=[jax.ShapeDtypeStruct((b, s, d), F32)] + c_shapes,
        scratch_shapes=[pltpu.VMEM((tm, d), BF16)],
        compiler_params=_params(("parallel", "parallel", "arbitrary")),
        name="mlp_final" if final_norm else "mlp",
    )(x, scale, shift, gate, w1, w2, final_g, *[stacked for stacked, _, _ in casts])
    return out[0], out[1:]


def _cumsum_rows(x, tril):
    x1 = x.astype(BF16)
    r1 = x - x1.astype(F32)
    x2 = r1.astype(BF16)
    x3 = (r1 - x2.astype(F32)).astype(BF16)
    d = lambda t: jnp.dot(tril, t, preferred_element_type=F32)
    return d(x1) + d(x2) + d(x3)


def _cummax_rows(x):
    n = x.shape[0]
    row = lax.broadcasted_iota(jnp.int32, x.shape, 0)
    sh = 1
    while sh < n:
        if sh < V7X_SUBLANES:
            prev = jnp.where(row >= sh, pltpu.roll(x, sh, 0), -jnp.inf)
        else:
            prev = jnp.concatenate([jnp.full((sh, x.shape[1]), -jnp.inf, x.dtype), x[:n - sh]], axis=0)
        x = jnp.maximum(x, prev)
        sh *= 2
    return x


def _log_sigmoid(z):
    return jnp.minimum(z, 0.0) - jnp.log1p(jnp.exp(-jnp.abs(z)))


def _mlstm_kernel(q_ref, k_ref, v_ref, o_ref, g_ref, bias_ref, ng_ref, y_ref, cn_ref, m_ref,
                  *, rows, heads, dqk, dv):
    L, LN = LSTM_CHUNK, V7X_LANES

    @pl.when(pl.program_id(1) == 0)
    def _():
        cn_ref[...] = jnp.zeros_like(cn_ref)
        m_ref[...] = jnp.zeros_like(m_ref)

    row = lax.broadcasted_iota(jnp.int32, (L, L), 0)
    col = lax.broadcasted_iota(jnp.int32, (L, L), 1)
    causal = row >= col
    tril = jnp.where(causal, 1.0, 0.0).astype(BF16)
    eye = jnp.where(lax.broadcasted_iota(jnp.int32, (dqk, dqk), 0)
                    == lax.broadcasted_iota(jnp.int32, (dqk, dqk), 1), 1.0, 0.0).astype(BF16)

    b, cm, m_row, decay_row, r_rows, wk_rows = [], [], [], [], [], []
    for bb in range(rows):
        log_in = g_ref[bb, :, 0:LN] + bias_ref[:, 0:LN]
        b.append(_cumsum_rows(_log_sigmoid(g_ref[bb, :, LN:2 * LN] + bias_ref[:, LN:2 * LN]), tril))
        r = log_in - b[bb]
        cm.append(_cummax_rows(r))
        m_row.append(m_ref[bb:bb + 1, :])
        mx_last = jnp.maximum(m_row[bb], cm[bb][L - 1:L, :])
        decay_row.append(jnp.exp(m_row[bb] - mx_last))
        wk = jnp.exp(r - mx_last)
        m_ref[bb:bb + 1, :] = b[bb][L - 1:L, :] + mx_last
        r_rows.append(r.T)
        wk_rows.append(wk.T)

    floor_scale = float(dqk) ** 0.5
    ones_blk = jnp.ones((L, LN), BF16)
    nt = (((1,), (1,)), ((), ()))
    units = [(bb, h) for bb in range(rows) for h in range(heads)]
    ur = range(len(units))
    q = [q_ref[bb, :, h * dqk:(h + 1) * dqk] for bb, h in units]
    k = [k_ref[bb, :, h * dqk:(h + 1) * dqk] for bb, h in units]
    s = [lax.dot_general(q[u], k[u], nt, preferred_element_type=F32) for u in ur]
    k_t = [lax.dot_general(eye, k[u], nt, preferred_element_type=F32) for u in ur]

    lhs, rhs, vo, floor = [], [], [], []
    for u, (bb, h) in enumerate(units):
        m_h = m_row[bb][:, h:h + 1]
        mx = jnp.maximum(jnp.broadcast_to(cm[bb][:, h:h + 1], (L, LN)), m_h)
        b_t = jnp.broadcast_to(b[bb][:, h:h + 1], (L, LN))
        p = jnp.exp(jnp.where(causal, r_rows[bb][h:h + 1, :] - mx, -jnp.inf)) * s[u]
        wq = (q[u].astype(F32) * jnp.exp(m_h - mx)).astype(BF16)
        lhs.append(jnp.concatenate([wq, p.astype(BF16)], axis=1))
        vo.append(jnp.concatenate([v_ref[bb, :, h * dv:(h + 1) * dv], ones_blk], axis=1))
        rhs.append(jnp.concatenate([cn_ref[u].astype(BF16), vo[u]], axis=0))
        floor.append(jnp.exp(-(b_t + mx)) * floor_scale)

    out = [jnp.dot(lhs[u], rhs[u], preferred_element_type=F32) for u in ur]

    for u, (bb, h) in enumerate(units):
        kw_t = (k_t[u] * wk_rows[bb][h:h + 1, :]).astype(BF16)
        cn_ref[u] = (decay_row[bb][:, h:h + 1] * cn_ref[u]
                     + jnp.dot(kw_t, vo[u], preferred_element_type=F32))

    for u, (bb, h) in enumerate(units):
        inv = 1.0 / jnp.maximum(jnp.abs(out[u][:, dv:]), floor[u])
        hh = out[u][:, :dv] * jnp.concatenate([inv] * (dv // LN), axis=1)
        ms = jnp.mean(hh * hh, axis=1, keepdims=True)
        hn = (hh * lax.rsqrt(ms + EPS)) * ng_ref[:, h * dv:(h + 1) * dv]
        og = 0.5 * jnp.tanh(0.5 * o_ref[bb, :, h * dv:(h + 1) * dv].astype(F32)) + 0.5
        y_ref[bb, :, h * dv:(h + 1) * dv] = (hn * og).astype(y_ref.dtype)


def _mlstm(proj, gates, gate_bias, norm_g, heads, dqk, dv):
    b, s, _ = proj.shape
    L = LSTM_CHUNK
    qk_w, v_w = heads * dqk, heads * dv
    rows = 4 if b % 4 == 0 else 1
    assert v_w == 2 * qk_w and dqk == L and heads <= V7X_LANES and rows <= V7X_SUBLANES
    kernel = functools.partial(_mlstm_kernel, rows=rows, heads=heads, dqk=dqk, dv=dv)
    return pl.pallas_call(
        kernel, grid=(b // rows, s // L),
        in_specs=[
            pl.BlockSpec((rows, L, qk_w), lambda bi, c: (bi, c, 0)),
            pl.BlockSpec((rows, L, qk_w), lambda bi, c: (bi, c, 1)),
            pl.BlockSpec((rows, L, v_w), lambda bi, c: (bi, c, 1)),
            pl.BlockSpec((rows, L, v_w), lambda bi, c: (bi, c, 2)),
            pl.BlockSpec((rows, L, gates.shape[2]), lambda bi, c: (bi, c, 0)),
            pl.BlockSpec((1, gate_bias.shape[1]), lambda bi, c: (0, 0)),
            pl.BlockSpec((1, v_w), lambda bi, c: (0, 0)),
        ],
        out_specs=pl.BlockSpec((rows, L, v_w), lambda bi, c: (bi, c, 0)),
        out_shape=jax.ShapeDtypeStruct((b, s, v_w), BF16),
        scratch_shapes=[pltpu.VMEM((rows * heads, dqk, dv + V7X_LANES), F32),
                        pltpu.VMEM((V7X_SUBLANES, V7X_LANES), F32)],
        compiler_params=_params(("parallel", "arbitrary")),
        name="mlstm",
    )(proj, proj, proj, proj, gates, gate_bias, norm_g)


def _scan_permutation(tt):
    seg = tt // V7X_SUBLANES
    r = jnp.arange(tt)
    src = (r % V7X_SUBLANES) * seg + r // V7X_SUBLANES
    to_interleaved = (src[:, None] == jnp.arange(tt)[None, :]).astype(BF16)
    return to_interleaved, to_interleaved.T


def _sqrt_nonneg(x):
    return jnp.where(x > 0.0, x * lax.rsqrt(x), 0.0)


def _gelu_tanh(x):
    c = (2.0 / jnp.pi) ** 0.5
    hx = 0.5 * x
    return hx + hx * jnp.tanh(x * (c + (c * 0.044715) * (x * x)))


def _rglru_layer_kernel(*refs, tt, nblk, bw, n_cast):
    (x_ref, sc_ref, sh_ref, g_ref, win_ref, wout_ref, pm_ref, pmt_ref, cw_ref, cb_ref,
     wra_ref, bra_ref, wri_ref, bri_ref, lam_ref) = refs[:15]
    cast_src = refs[15:15 + n_cast]
    o_ref = refs[15 + n_cast]
    cast_dst = refs[16 + n_cast:16 + 2 * n_cast]
    hp_ref, y_ref, gp_ref, a_ref, u_ref, tail_ref, carry_ref = refs[16 + 2 * n_cast:]
    _run_casts(cast_src, cast_dst)

    S8 = V7X_SUBLANES
    seg = tt // S8
    H = CONV_W - 1
    d = a_ref.shape[1]

    @pl.when(pl.program_id(1) == 0)
    def _():
        tail_ref[...] = jnp.zeros_like(tail_ref)
        carry_ref[...] = jnp.zeros_like(carry_ref)

    h = _mod_norm(x_ref[0], sc_ref[0], sh_ref[0]).astype(BF16)
    hp_ref[...] = jnp.dot(pm_ref[...], h, preferred_element_type=F32).astype(BF16)

    lam = lam_ref[...]
    half_nsp = (-0.5 * RG_C) * (jnp.maximum(-lam, 0.0) + jnp.log1p(jnp.exp(-jnp.abs(lam))))
    first_segment = lax.broadcasted_iota(jnp.int32, (S8, bw), 0) == 0

    def in_proj(first_col, group):
        cols = slice(first_col + group * GROUP * bw, first_col + (group + 1) * GROUP * bw)
        return jnp.dot(hp_ref[...], win_ref[:, cols], preferred_element_type=F32)

    def project_gate(group):
        gp_ref[:, group * GROUP * bw:(group + 1) * GROUP * bw] = in_proj(d, group)

    GROUP = 2
    early_gate_groups = nblk // GROUP // 2
    ahead = in_proj(0, 0)
    for nb in range(nblk):
        sl = slice(nb * bw, (nb + 1) * bw)
        if nb % GROUP == 0:
            group_out = ahead
            if nb + GROUP < nblk:
                ahead = in_proj(0, nb // GROUP + 1)
            if nb // GROUP < early_gate_groups:
                project_gate(nb // GROUP)
        xp = group_out[:, (nb % GROUP) * bw:(nb % GROUP + 1) * bw]
        hist = []
        for g in range(H):
            cur = xp[(seg - H + g) * S8:(seg - H + g + 1) * S8, :]
            prev = tail_ref[g * S8:(g + 1) * S8, sl]
            hist.append(jnp.where(first_segment, pltpu.roll(prev, 1, 0), pltpu.roll(cur, 1, 0)))
        tail_ref[:, sl] = xp[(seg - H) * S8:, :]
        xext = jnp.concatenate(hist + [xp], axis=0)
        xc = cb_ref[:, sl] + cw_ref[H:H + 1, sl] * xp
        for k in range(H):
            xc = xc + cw_ref[k:k + 1, sl] * xext[k * S8:k * S8 + tt, :]

        xc16 = xc.astype(BF16)
        t_r = jnp.tanh(jnp.dot(xc16, wra_ref[nb], preferred_element_type=F32) + bra_ref[:, sl])
        i = 0.5 * jnp.tanh(jnp.dot(xc16, wri_ref[nb], preferred_element_type=F32) + bri_ref[:, sl]) + 0.5
        log_a = t_r * half_nsp[:, sl] + half_nsp[:, sl]
        a = jnp.exp(log_a)
        one_m_a2 = jnp.tanh(log_a) * (-1.0 - a * a)
        a_ref[:, sl] = a
        u_ref[:, sl] = _sqrt_nonneg(one_m_a2) * (i * xc)

    for group in range(early_gate_groups, nblk // GROUP):
        project_gate(group)
    h_end, p_end = jnp.zeros((S8, d), F32), jnp.ones((S8, d), F32)
    for j in range(seg):
        rows = slice(j * S8, (j + 1) * S8)
        a = a_ref[rows, :]
        h_end = a * h_end + u_ref[rows, :]
        p_end = a * p_end
        u_ref[rows, :] = h_end
        a_ref[rows, :] = p_end

    c = carry_ref[0:1, :]
    carry_rows = []
    for sgi in range(S8):
        carry_rows.append(c)
        c = h_end[sgi:sgi + 1, :] + p_end[sgi:sgi + 1, :] * c
    carry_ref[0:1, :] = c
    carry_in = jnp.concatenate(carry_rows, axis=0)

    pmt = pmt_ref[...]

    def out_proj(group):
        gsl = slice(group * GROUP * bw, (group + 1) * GROUP * bw)
        part = g_ref[0] * jnp.dot(y_ref[:, gsl], wout_ref[gsl, :], preferred_element_type=F32)
        if group == 0:
            o_ref[0] = x_ref[0] + part
        else:
            o_ref[0] += part

    for nb in range(nblk):
        sl = slice(nb * bw, (nb + 1) * bw)
        cin = jnp.concatenate([carry_in[:, sl]] * seg, axis=0)
        hs = u_ref[:, sl] + a_ref[:, sl] * cin
        yi = (hs * _gelu_tanh(gp_ref[:, sl])).astype(BF16)
        y_ref[:, sl] = jnp.dot(pmt, yi, preferred_element_type=F32).astype(BF16)
        if nb % GROUP == GROUP - 1 and nb >= 2 * GROUP - 1:
            out_proj(nb // GROUP - 1)
    out_proj(nblk // GROUP - 1)


def _rglru_layer(x, scale, shift, gate, w_in, w_out, conv_w, conv_b, w_ra, b_ra, w_ri, b_ri, lam,
                 casts=(), tt=256):
    b, s, dm = x.shape
    d = w_out.shape[0]
    nblk, bw, _ = w_ra.shape
    nt = s // tt
    pm, pmt = _scan_permutation(tt)
    const = lambda shape: pl.BlockSpec(shape, lambda bi, i: (0,) * len(shape))
    resident = lambda shape: pl.BlockSpec(shape, lambda bi, i: (0,) * len(shape),
                                          pipeline_mode=pl.Buffered(1))
    mod_spec = pl.BlockSpec((1, 1, dm), lambda bi, i: (bi, 0, 0))
    c_in, c_out, c_shapes = _cast_specs(casts, b * nt, lambda bi, i: bi * nt + i)
    kernel = functools.partial(_rglru_layer_kernel, tt=tt, nblk=nblk, bw=bw, n_cast=len(casts))
    out = pl.pallas_call(
        kernel, grid=(b, nt),
        in_specs=[
            pl.BlockSpec((1, tt, dm), lambda bi, i: (bi, i, 0)),
            mod_spec, mod_spec, mod_spec,
            resident((dm, 2 * d)), resident((d, dm)),
            const((tt, tt)), const((tt, tt)),
            const((CONV_W, d)), const((1, d)),
            const((nblk, bw, bw)), const((1, d)), const((nblk, bw, bw)), const((1, d)),
            const((1, d)),
        ] + c_in,
        out_specs=[pl.BlockSpec((1, tt, dm), lambda bi, i: (bi, i, 0))] + c_out,
        out_shape=[jax.ShapeDtypeStruct((b, s, dm), F32)] + c_shapes,
        scratch_shapes=[pltpu.VMEM((tt, dm), BF16),
                        pltpu.VMEM((tt, d), BF16),
                        pltpu.VMEM((tt, d), F32),
                        pltpu.VMEM((tt, d), F32),
                        pltpu.VMEM((tt, d), F32),
                        pltpu.VMEM(((CONV_W - 1) * V7X_SUBLANES, d), F32),
                        pltpu.VMEM((V7X_SUBLANES, d), F32)],
        compiler_params=_params(("parallel", "arbitrary")),
        name="rglru_layer",
    )(x, scale, shift, gate, w_in, w_out, pm, pmt, conv_w, conv_b, w_ra, b_ra, w_ri, b_ri, lam,
      *[stacked for stacked, _, _ in casts])
    return out[0], out[1:]


def _gate_lanes(ig, fg):
    pad = [(0, 0)] * (ig.ndim - 1) + [(0, V7X_LANES - ig.shape[-1])]
    return jnp.concatenate([jnp.pad(ig, pad), jnp.pad(fg, pad)], axis=-1)


def kernel(x, c, ada_w, ada_b, a_w_in, a_b_gate, a_norm_g, a_w_out, b_w_in, b_conv_w, b_conv_b,
           b_w_ra, b_b_ra, b_w_ri, b_b_ri, b_lam, b_w_out, mlp_w1, mlp_w2, final_g):
    depth = ada_w.shape[0]
    d = x.shape[-1]
    heads = a_b_gate.shape[-1]
    v_w = a_norm_g.shape[-1]
    qk_w = (a_w_in.shape[-1] - 2 * v_w - 2 * heads) // 2
    dqk, dv = qk_w // heads, v_w // heads
    n_main = 2 * qk_w + 2 * v_w

    mod = _ada_mod(c, ada_w, ada_b)
    final_row = final_g.reshape(1, d)

    def site(layer, j):
        m = mod[layer * 2 + j]
        return (m[:, None, d:2 * d], m[:, None, 0:d], m[:, None, 2 * d:3 * d])

    def big_weights(layer):
        slot = layer // 2
        mixer = ((a_w_in, slot, n_main), (a_w_out, slot, d)) if layer % 2 == 0 else \
                ((b_w_in, slot, b_w_in.shape[-1]), (b_w_out, slot, d))
        return mixer + ((mlp_w1, layer, mlp_w1.shape[-1]), (mlp_w2, layer, d))

    first, lead, _ = big_weights(0)[0]
    w_in16 = first[lead].astype(BF16)
    next16 = None
    for layer in range(depth):
        slot = layer // 2
        casts = ()
        if layer == 0:
            casts = big_weights(0)[1:] + (big_weights(1)[:2] if depth > 1 else ())
        elif next16 is not None:
            w_in16, w_out16, *mlp16 = next16
            if mlp16:
                w1_16, w2_16 = mlp16
        scale, shift, gate = site(layer, 0)
        if layer % 2 == 0:
            w_g = a_w_in[slot][:, n_main:]
            w_gates = _gate_lanes(w_g[:, :heads], w_g[:, heads:]).astype(BF16)
            proj, gates, *converted = _norm_matmul(x, scale, shift, w_in16, w_gates, casts=casts,
                                                   n_cols=n_main)
            gate_bias = _gate_lanes(a_b_gate[slot, 0:1].astype(F32), a_b_gate[slot, 1:2].astype(F32))
            y = _mlstm(proj, gates, gate_bias, a_norm_g[slot].reshape(1, v_w).astype(F32),
                       heads, dqk, dv)
            if converted:
                w_out16, w1_16, w2_16 = converted[:3]
                next16 = converted[3:] or None
            x = _out_proj_residual(y, w_out16, x, gate)
        else:
            assert not casts
            drnn = b_lam.shape[-1]
            x, converted = _rglru_layer(
                x, scale, shift, gate, w_in16, w_out16,
                b_conv_w[slot], b_conv_b[slot].reshape(1, drnn),
                (0.5 * b_w_ra[slot]).astype(BF16), 0.5 * b_b_ra[slot].reshape(1, drnn),
                (0.5 * b_w_ri[slot]).astype(BF16), 0.5 * b_b_ri[slot].reshape(1, drnn),
                b_lam[slot].reshape(1, drnn),
                casts=big_weights(layer)[2:] if layer == 1 else ())
            if converted:
                w1_16, w2_16 = converted
        scale, shift, gate = site(layer, 1)
        last = layer == depth - 1
        host = 1 <= layer < depth - 1
        x, converted = _mlp(x, scale, shift, gate, w1_16, w2_16, final_row, final_norm=last,
                            casts=big_weights(layer + 1) if host else ())
        if converted:
            next16 = converted
    return x
```

```python
import functools

import jax
import jax.numpy as jnp
from jax import lax
from jax.experimental import pallas as pl
from jax.experimental.pallas import tpu as pltpu

EPS = 1e-6
RG_C = 8.0
CONV_W = 4
LSTM_CHUNK = 128
V7X_LANES = 128
V7X_SUBLANES = 8
VMEM_LIMIT = 60 * 1024 * 1024

F32 = jnp.float32
BF16 = jnp.bfloat16


def _params(sem):
    return pltpu.CompilerParams(dimension_semantics=sem, vmem_limit_bytes=VMEM_LIMIT)


def _mod_norm(x, scale, shift):
    ms = jnp.mean(x * x, axis=-1, keepdims=True)
    return (x * lax.rsqrt(ms + EPS)) * (1.0 + scale) + shift


BF16_SUBLANES = 16


def _cast_block(r, c, n_steps):
    col_blocks = 1
    while r * col_blocks < n_steps * BF16_SUBLANES:
        col_blocks *= 2
    assert (r * col_blocks) % n_steps == 0 and c % (col_blocks * V7X_LANES) == 0, (r, c, n_steps)
    return (r * col_blocks // n_steps, c // col_blocks), col_blocks


def _cast_specs(casts, n_steps, step_id):
    in_specs, out_specs, out_shapes = [], [], []
    for stacked, lead, cols in casts:
        r = stacked.shape[1]
        block, cb = _cast_block(r, cols, n_steps)
        src_index = lambda *ids, cb=cb, lead=lead: (lead, step_id(*ids) // cb, step_id(*ids) % cb)
        dst_index = lambda *ids, cb=cb: (step_id(*ids) // cb, step_id(*ids) % cb)
        in_specs.append(pl.BlockSpec((None,) + block, src_index))
        out_specs.append(pl.BlockSpec(block, dst_index))
        out_shapes.append(jax.ShapeDtypeStruct((r, cols), BF16))
    return in_specs, out_specs, out_shapes


def _run_casts(src_refs, dst_refs):
    for src, dst in zip(src_refs, dst_refs):
        dst[...] = src[...].astype(dst.dtype)


def _ada_kernel(c_ref, w_ref, b_ref, o_ref):
    c = c_ref[...]
    s = (c * jax.nn.sigmoid(c)).astype(BF16)
    o_ref[0] = jnp.dot(s, w_ref[0].astype(BF16), preferred_element_type=F32) + b_ref[0]


def _ada_mod(c, ada_w, ada_b, tn=1024):
    depth, two, d, n = ada_w.shape
    sites = depth * two
    b = c.shape[0]
    bp = -(-b // V7X_SUBLANES) * V7X_SUBLANES
    c_pad = jnp.pad(c, ((0, bp - b), (0, 0)))
    w = ada_w.reshape(sites, d, n)
    bias = ada_b.reshape(sites, 1, n)
    out = pl.pallas_call(
        _ada_kernel,
        grid=(sites, n // tn),
        in_specs=[
            pl.BlockSpec((bp, d), lambda s, j: (0, 0)),
            pl.BlockSpec((1, d, tn), lambda s, j: (s, 0, j)),
            pl.BlockSpec((1, 1, tn), lambda s, j: (s, 0, j)),
        ],
        out_specs=pl.BlockSpec((1, bp, tn), lambda s, j: (s, 0, j)),
        out_shape=jax.ShapeDtypeStruct((sites, bp, n), F32),
        compiler_params=_params(("parallel", "parallel")),
        name="ada_mod",
    )(c_pad, w, bias)
    return out[:, :b, :]


def _norm_matmul_kernel(*refs, has_gates, n_cast, cast_j):
    n_in = 4 + has_gates
    x_ref, sc_ref, sh_ref, w_ref = refs[:4]
    cast_src = refs[n_in:n_in + n_cast]
    outs = refs[n_in + n_cast:-1]
    o_ref, cast_dst, h_ref = outs[0], outs[1 + has_gates:], refs[-1]
    j = pl.program_id(2)

    def project():
        o_ref[0] = jnp.dot(h_ref[...], w_ref[...], preferred_element_type=F32).astype(o_ref.dtype)

    @pl.when(j == 0)
    def _():
        h_ref[...] = _mod_norm(x_ref[0], sc_ref[0], sh_ref[0]).astype(BF16)
        if has_gates:
            outs[1][0] = jnp.dot(h_ref[...], refs[4][...], preferred_element_type=F32)
        project()

    pl.when(j > 0)(project)

    if n_cast:
        pl.when(j < cast_j)(lambda: _run_casts(cast_src, cast_dst))


def _norm_matmul(x, scale, shift, w, w_gates=None, casts=(), n_cols=None, tm=1024, tn=1024):
    b, s, d = x.shape
    n = w.shape[1] if n_cols is None else n_cols
    nj = n // tn
    grid = (b, s // tm, nj)
    cast_j = 1 << (nj.bit_length() - 1)
    step_id = lambda bi, i, j: (bi * (s // tm) + i) * cast_j + jnp.minimum(j, cast_j - 1)
    c_in, c_out, c_shapes = _cast_specs(casts, b * (s // tm) * cast_j, step_id)
    mod_spec = pl.BlockSpec((1, 1, d), lambda bi, i, j: (bi, 0, 0))
    in_specs = [pl.BlockSpec((1, tm, d), lambda bi, i, j: (bi, i, 0)), mod_spec, mod_spec,
                pl.BlockSpec((d, tn), lambda bi, i, j: (0, j))]
    out_specs = [pl.BlockSpec((1, tm, tn), lambda bi, i, j: (bi, i, j))]
    out_shapes = [jax.ShapeDtypeStruct((b, s, n), BF16)]
    args = [x, scale, shift, w]
    if w_gates is not None:
        ng = w_gates.shape[1]
        in_specs.append(pl.BlockSpec((d, ng), lambda bi, i, j: (0, 0)))
        out_specs.append(pl.BlockSpec((1, tm, ng), lambda bi, i, j: (bi, i, 0)))
        out_shapes.append(jax.ShapeDtypeStruct((b, s, ng), F32))
        args.append(w_gates)
    kernel = functools.partial(_norm_matmul_kernel, has_gates=w_gates is not None,
                               n_cast=len(casts), cast_j=cast_j)
    return pl.pallas_call(
        kernel, grid=grid,
        in_specs=in_specs + c_in, out_specs=out_specs + c_out, out_shape=out_shapes + c_shapes,
        scratch_shapes=[pltpu.VMEM((tm, d), BF16)],
        compiler_params=_params(("parallel", "parallel", "arbitrary")),
        name="norm_matmul_gates" if w_gates is not None else "norm_matmul",
    )(*args, *[stacked for stacked, _, _ in casts])


def _out_proj_kernel(y_ref, w_ref, x_ref, g_ref, o_ref):
    acc = jnp.dot(y_ref[0], w_ref[...], preferred_element_type=F32)
    o_ref[0] = x_ref[0] + g_ref[0] * acc


def _out_proj_residual(y, w, x, gate, tm=512, tn=2048):
    b, s, k = y.shape
    n = w.shape[1]
    return pl.pallas_call(
        _out_proj_kernel, grid=(b, s // tm, n // tn),
        in_specs=[
            pl.BlockSpec((1, tm, k), lambda bi, i, j: (bi, i, 0)),
            pl.BlockSpec((k, tn), lambda bi, i, j: (0, j)),
            pl.BlockSpec((1, tm, tn), lambda bi, i, j: (bi, i, j)),
            pl.BlockSpec((1, 1, tn), lambda bi, i, j: (bi, 0, j)),
        ],
        out_specs=pl.BlockSpec((1, tm, tn), lambda bi, i, j: (bi, i, j)),
        out_shape=jax.ShapeDtypeStruct((b, s, n), F32),
        compiler_params=_params(("parallel", "parallel", "arbitrary")),
        name="out_proj_residual",
    )(y, w, x, gate)


def _mlp_kernel(*refs, final_norm, n_cast):
    x_ref, sc_ref, sh_ref, g_ref, w1_ref, w2_ref, fg_ref = refs[:7]
    cast_src = refs[7:7 + n_cast]
    o_ref, cast_dst, h_ref = refs[7 + n_cast], refs[8 + n_cast:-1], refs[-1]
    f = pl.program_id(2)

    def ffn_slice():
        u = jnp.dot(h_ref[...], w1_ref[...], preferred_element_type=F32)
        u = jnp.square(jnp.maximum(u, 0.0)).astype(BF16)
        return g_ref[0] * jnp.dot(u, w2_ref[...], preferred_element_type=F32)

    @pl.when(f == 0)
    def _():
        h_ref[...] = _mod_norm(x_ref[0], sc_ref[0], sh_ref[0]).astype(BF16)
        o_ref[0] = x_ref[0] + ffn_slice()

    @pl.when(f > 0)
    def _():
        o_ref[0] += ffn_slice()

    _run_casts(cast_src, cast_dst)

    if final_norm:
        @pl.when(f == pl.num_programs(2) - 1)
        def _():
            y = o_ref[0]
            ms = jnp.mean(y * y, axis=-1, keepdims=True)
            o_ref[0] = (y * lax.rsqrt(ms + EPS)) * fg_ref[...]


def _mlp(x, scale, shift, gate, w1, w2, final_g, final_norm, casts=(), tm=1024, tf=None):
    b, s, d = x.shape
    dff = w1.shape[1]
    if tf is None:
        tf = 512 if casts else 1024
    ni, nf = s // tm, dff // tf
    c_in, c_out, c_shapes = _cast_specs(casts, b * ni * nf, lambda bi, i, f: (bi * ni + i) * nf + f)
    mod_spec = pl.BlockSpec((1, 1, d), lambda bi, i, f: (bi, 0, 0))
    out = pl.pallas_call(
        functools.partial(_mlp_kernel, final_norm=final_norm, n_cast=len(casts)),
        grid=(b, ni, nf),
        in_specs=[
            pl.BlockSpec((1, tm, d), lambda bi, i, f: (bi, i, 0)),
            mod_spec, mod_spec, mod_spec,
            pl.BlockSpec((d, tf), lambda bi, i, f: (0, f)),
            pl.BlockSpec((tf, d), lambda bi, i, f: (f, 0)),
            pl.BlockSpec((1, d), lambda bi, i, f: (0, 0)),
        ] + c_in,
        out_specs=[pl.BlockSpec((1, tm, d), lambda bi, i, f: (bi, i, 0))] + c_out,
        out_shape=[jax.ShapeDtypeStruct((b, s, d), F32)] + c_shapes,
        scratch_shapes=[pltpu.VMEM((tm, d), BF16)],
        compiler_params=_params(("parallel", "parallel", "arbitrary")),
        name="mlp_final" if final_norm else "mlp",
    )(x, scale, shift, gate, w1, w2, final_g, *[stacked for stacked, _, _ in casts])
    return out[0], out[1:]


def _cumsum_rows(x, tril):
    x1 = x.astype(BF16)
    r1 = x - x1.astype(F32)
    x2 = r1.astype(BF16)
    x3 = (r1 - x2.astype(F32)).astype(BF16)
    d = lambda t: jnp.dot(tril, t, preferred_element_type=F32)
    return d(x1) + d(x2) + d(x3)


def _cummax_rows(x):
    n = x.shape[0]
    row = lax.broadcasted_iota(jnp.int32, x.shape, 0)
    sh = 1
    while sh < n:
        if sh < V7X_SUBLANES:
            prev = jnp.where(row >= sh, pltpu.roll(x, sh, 0), -jnp.inf)
        else:
            prev = jnp.concatenate([jnp.full((sh, x.shape[1]), -jnp.inf, x.dtype), x[:n - sh]], axis=0)
        x = jnp.maximum(x, prev)
        sh *= 2
    return x


def _log_sigmoid(z):
    return jnp.minimum(z, 0.0) - jnp.log1p(jnp.exp(-jnp.abs(z)))


def _mlstm_kernel(q_ref, k_ref, v_ref, o_ref, g_ref, bias_ref, ng_ref, y_ref, cn_ref, m_ref,
                  *, rows, heads, dqk, dv):
    L, LN = LSTM_CHUNK, V7X_LANES

    @pl.when(pl.program_id(1) == 0)
    def _():
        cn_ref[...] = jnp.zeros_like(cn_ref)
        m_ref[...] = jnp.zeros_like(m_ref)

    row = lax.broadcasted_iota(jnp.int32, (L, L), 0)
    col = lax.broadcasted_iota(jnp.int32, (L, L), 1)
    causal = row >= col
    tril = jnp.where(causal, 1.0, 0.0).astype(BF16)
    eye = jnp.where(lax.broadcasted_iota(jnp.int32, (dqk, dqk), 0)
                    == lax.broadcasted_iota(jnp.int32, (dqk, dqk), 1), 1.0, 0.0).astype(BF16)

    b, cm, m_row, decay_row, r_rows, wk_rows = [], [], [], [], [], []
    for bb in range(rows):
        log_in = g_ref[bb, :, 0:LN] + bias_ref[:, 0:LN]
        b.append(_cumsum_rows(_log_sigmoid(g_ref[bb, :, LN:2 * LN] + bias_ref[:, LN:2 * LN]), tril))
        r = log_in - b[bb]
        cm.append(_cummax_rows(r))
        m_row.append(m_ref[bb:bb + 1, :])
        mx_last = jnp.maximum(m_row[bb], cm[bb][L - 1:L, :])
        decay_row.append(jnp.exp(m_row[bb] - mx_last))
        wk = jnp.exp(r - mx_last)
        m_ref[bb:bb + 1, :] = b[bb][L - 1:L, :] + mx_last
        r_rows.append(r.T)
        wk_rows.append(wk.T)

    floor_scale = float(dqk) ** 0.5
    ones_blk = jnp.ones((L, LN), BF16)
    nt = (((1,), (1,)), ((), ()))
    units = [(bb, h) for bb in range(rows) for h in range(heads)]
    ur = range(len(units))
    q = [q_ref[bb, :, h * dqk:(h + 1) * dqk] for bb, h in units]
    k = [k_ref[bb, :, h * dqk:(h + 1) * dqk] for bb, h in units]
    s = [lax.dot_general(q[u], k[u], nt, preferred_element_type=F32) for u in ur]
    k_t = [lax.dot_general(eye, k[u], nt, preferred_element_type=F32) for u in ur]

    lhs, rhs, vo, floor = [], [], [], []
    for u, (bb, h) in enumerate(units):
        m_h = m_row[bb][:, h:h + 1]
        mx = jnp.maximum(jnp.broadcast_to(cm[bb][:, h:h + 1], (L, LN)), m_h)
        b_t = jnp.broadcast_to(b[bb][:, h:h + 1], (L, LN))
        p = jnp.exp(jnp.where(causal, r_rows[bb][h:h + 1, :] - mx, -jnp.inf)) * s[u]
        wq = (q[u].astype(F32) * jnp.exp(m_h - mx)).astype(BF16)
        lhs.append(jnp.concatenate([wq, p.astype(BF16)], axis=1))
        vo.append(jnp.concatenate([v_ref[bb, :, h * dv:(h + 1) * dv], ones_blk], axis=1))
        rhs.append(jnp.concatenate([cn_ref[u].astype(BF16), vo[u]], axis=0))
        floor.append(jnp.exp(-(b_t + mx)) * floor_scale)

    out = [jnp.dot(lhs[u], rhs[u], preferred_element_type=F32) for u in ur]

    for u, (bb, h) in enumerate(units):
        kw_t = (k_t[u] * wk_rows[bb][h:h + 1, :]).astype(BF16)
        cn_ref[u] = (decay_row[bb][:, h:h + 1] * cn_ref[u]
                     + jnp.dot(kw_t, vo[u], preferred_element_type=F32))

    for u, (bb, h) in enumerate(units):
        inv = 1.0 / jnp.maximum(jnp.abs(out[u][:, dv:]), floor[u])
        hh = out[u][:, :dv] * jnp.concatenate([inv] * (dv // LN), axis=1)
        ms = jnp.mean(hh * hh, axis=1, keepdims=True)
        hn = (hh * lax.rsqrt(ms + EPS)) * ng_ref[:, h * dv:(h + 1) * dv]
        og = 0.5 * jnp.tanh(0.5 * o_ref[bb, :, h * dv:(h + 1) * dv].astype(F32)) + 0.5
        y_ref[bb, :, h * dv:(h + 1) * dv] = (hn * og).astype(y_ref.dtype)


def _mlstm(proj, gates, gate_bias, norm_g, heads, dqk, dv):
    b, s, _ = proj.shape
    L = LSTM_CHUNK
    qk_w, v_w = heads * dqk, heads * dv
    rows = 4 if b % 4 == 0 else 1
    assert v_w == 2 * qk_w and dqk == L and heads <= V7X_LANES and rows <= V7X_SUBLANES
    kernel = functools.partial(_mlstm_kernel, rows=rows, heads=heads, dqk=dqk, dv=dv)
    return pl.pallas_call(
        kernel, grid=(b // rows, s // L),
        in_specs=[
            pl.BlockSpec((rows, L, qk_w), lambda bi, c: (bi, c, 0)),
            pl.BlockSpec((rows, L, qk_w), lambda bi, c: (bi, c, 1)),
            pl.BlockSpec((rows, L, v_w), lambda bi, c: (bi, c, 1)),
            pl.BlockSpec((rows, L, v_w), lambda bi, c: (bi, c, 2)),
            pl.BlockSpec((rows, L, gates.shape[2]), lambda bi, c: (bi, c, 0)),
            pl.BlockSpec((1, gate_bias.shape[1]), lambda bi, c: (0, 0)),
            pl.BlockSpec((1, v_w), lambda bi, c: (0, 0)),
        ],
        out_specs=pl.BlockSpec((rows, L, v_w), lambda bi, c: (bi, c, 0)),
        out_shape=jax.ShapeDtypeStruct((b, s, v_w), BF16),
        scratch_shapes=[pltpu.VMEM((rows * heads, dqk, dv + V7X_LANES), F32),
                        pltpu.VMEM((V7X_SUBLANES, V7X_LANES), F32)],
        compiler_params=_params(("parallel", "arbitrary")),
        name="mlstm",
    )(proj, proj, proj, proj, gates, gate_bias, norm_g)


def _scan_permutation(tt):
    seg = tt // V7X_SUBLANES
    r = jnp.arange(tt)
    src = (r % V7X_SUBLANES) * seg + r // V7X_SUBLANES
    to_interleaved = (src[:, None] == jnp.arange(tt)[None, :]).astype(BF16)
    return to_interleaved, to_interleaved.T


def _sqrt_nonneg(x):
    return jnp.where(x > 0.0, x * lax.rsqrt(x), 0.0)


def _gelu_tanh(x):
    c = (2.0 / jnp.pi) ** 0.5
    hx = 0.5 * x
    return hx + hx * jnp.tanh(x * (c + (c * 0.044715) * (x * x)))


def _rglru_layer_kernel(*refs, tt, nblk, bw, n_cast):
    (x_ref, sc_ref, sh_ref, g_ref, win_ref, wout_ref, pm_ref, pmt_ref, cw_ref, cb_ref,
     wra_ref, bra_ref, wri_ref, bri_ref, lam_ref) = refs[:15]
    cast_src = refs[15:15 + n_cast]
    o_ref = refs[15 + n_cast]
    cast_dst = refs[16 + n_cast:16 + 2 * n_cast]
    hp_ref, y_ref, gp_ref, a_ref, u_ref, tail_ref, carry_ref = refs[16 + 2 * n_cast:]
    _run_casts(cast_src, cast_dst)

    S8 = V7X_SUBLANES
    seg = tt // S8
    H = CONV_W - 1
    d = a_ref.shape[1]

    @pl.when(pl.program_id(1) == 0)
    def _():
        tail_ref[...] = jnp.zeros_like(tail_ref)
        carry_ref[...] = jnp.zeros_like(carry_ref)

    h = _mod_norm(x_ref[0], sc_ref[0], sh_ref[0]).astype(BF16)
    hp_ref[...] = jnp.dot(pm_ref[...], h, preferred_element_type=F32).astype(BF16)

    lam = lam_ref[...]
    half_nsp = (-0.5 * RG_C) * (jnp.maximum(-lam, 0.0) + jnp.log1p(jnp.exp(-jnp.abs(lam))))
    first_segment = lax.broadcasted_iota(jnp.int32, (S8, bw), 0) == 0

    def in_proj(first_col, group):
        cols = slice(first_col + group * GROUP * bw, first_col + (group + 1) * GROUP * bw)
        return jnp.dot(hp_ref[...], win_ref[:, cols], preferred_element_type=F32)

    def project_gate(group):
        gp_ref[:, group * GROUP * bw:(group + 1) * GROUP * bw] = in_proj(d, group)

    GROUP = 2
    early_gate_groups = nblk // GROUP // 2
    ahead = in_proj(0, 0)
    for nb in range(nblk):
        sl = slice(nb * bw, (nb + 1) * bw)
        if nb % GROUP == 0:
            group_out = ahead
            if nb + GROUP < nblk:
                ahead = in_proj(0, nb // GROUP + 1)
            if nb // GROUP < early_gate_groups:
                project_gate(nb // GROUP)
        xp = group_out[:, (nb % GROUP) * bw:(nb % GROUP + 1) * bw]
        hist = []
        for g in range(H):
            cur = xp[(seg - H + g) * S8:(seg - H + g + 1) * S8, :]
            prev = tail_ref[g * S8:(g + 1) * S8, sl]
            hist.append(jnp.where(first_segment, pltpu.roll(prev, 1, 0), pltpu.roll(cur, 1, 0)))
        tail_ref[:, sl] = xp[(seg - H) * S8:, :]
        xext = jnp.concatenate(hist + [xp], axis=0)
        xc = cb_ref[:, sl] + cw_ref[H:H + 1, sl] * xp
        for k in range(H):
            xc = xc + cw_ref[k:k + 1, sl] * xext[k * S8:k * S8 + tt, :]

        xc16 = xc.astype(BF16)
        t_r = jnp.tanh(jnp.dot(xc16, wra_ref[nb], preferred_element_type=F32) + bra_ref[:, sl])
        i = 0.5 * jnp.tanh(jnp.dot(xc16, wri_ref[nb], preferred_element_type=F32) + bri_ref[:, sl]) + 0.5
        log_a = t_r * half_nsp[:, sl] + half_nsp[:, sl]
        a = jnp.exp(log_a)
        one_m_a2 = jnp.tanh(log_a) * (-1.0 - a * a)
        a_ref[:, sl] = a
        u_ref[:, sl] = _sqrt_nonneg(one_m_a2) * (i * xc)

    for group in range(early_gate_groups, nblk // GROUP):
        project_gate(group)
    h_end, p_end = jnp.zeros((S8, d), F32), jnp.ones((S8, d), F32)
    for j in range(seg):
        rows = slice(j * S8, (j + 1) * S8)
        a = a_ref[rows, :]
        h_end = a * h_end + u_ref[rows, :]
        p_end = a * p_end
        u_ref[rows, :] = h_end
        a_ref[rows, :] = p_end

    c = carry_ref[0:1, :]
    carry_rows = []
    for sgi in range(S8):
        carry_rows.append(c)
        c = h_end[sgi:sgi + 1, :] + p_end[sgi:sgi + 1, :] * c
    carry_ref[0:1, :] = c
    carry_in = jnp.concatenate(carry_rows, axis=0)

    pmt = pmt_ref[...]

    def out_proj(group):
        gsl = slice(group * GROUP * bw, (group + 1) * GROUP * bw)
        part = g_ref[0] * jnp.dot(y_ref[:, gsl], wout_ref[gsl, :], preferred_element_type=F32)
        if group == 0:
            o_ref[0] = x_ref[0] + part
        else:
            o_ref[0] += part

    for nb in range(nblk):
        sl = slice(nb * bw, (nb + 1) * bw)
        cin = jnp.concatenate([carry_in[:, sl]] * seg, axis=0)
        hs = u_ref[:, sl] + a_ref[:, sl] * cin
        yi = (hs * _gelu_tanh(gp_ref[:, sl])).astype(BF16)
        y_ref[:, sl] = jnp.dot(pmt, yi, preferred_element_type=F32).astype(BF16)
        if nb % GROUP == GROUP - 1 and nb >= 2 * GROUP - 1:
            out_proj(nb // GROUP - 1)
    out_proj(nblk // GROUP - 1)


def _rglru_layer(x, scale, shift, gate, w_in, w_out, conv_w, conv_b, w_ra, b_ra, w_ri, b_ri, lam,
                 casts=(), tt=256):
    b, s, dm = x.shape
    d = w_out.shape[0]
    nblk, bw, _ = w_ra.shape
    nt = s // tt
    pm, pmt = _scan_permutation(tt)
    const = lambda shape: pl.BlockSpec(shape, lambda bi, i: (0,) * len(shape))
    resident = lambda shape: pl.BlockSpec(shape, lambda bi, i: (0,) * len(shape),
                                          pipeline_mode=pl.Buffered(1))
    mod_spec = pl.BlockSpec((1, 1, dm), lambda bi, i: (bi, 0, 0))
    c_in, c_out, c_shapes = _cast_specs(casts, b * nt, lambda bi, i: bi * nt + i)
    kernel = functools.partial(_rglru_layer_kernel, tt=tt, nblk=nblk, bw=bw, n_cast=len(casts))
    out = pl.pallas_call(
        kernel, grid=(b, nt),
        in_specs=[
            pl.BlockSpec((1, tt, dm), lambda bi, i: (bi, i, 0)),
            mod_spec, mod_spec, mod_spec,
            resident((dm, 2 * d)), resident((d, dm)),
            const((tt, tt)), const((tt, tt)),
            const((CONV_W, d)), const((1, d)),
            const((nblk, bw, bw)), const((1, d)), const((nblk, bw, bw)), const((1, d)),
            const((1, d)),
        ] + c_in,
        out_specs=[pl.BlockSpec((1, tt, dm), lambda bi, i: (bi, i, 0))] + c_out,
        out_shape=[jax.ShapeDtypeStruct((b, s, dm), F32)] + c_shapes,
        scratch_shapes=[pltpu.VMEM((tt, dm), BF16),
                        pltpu.VMEM((tt, d), BF16),
                        pltpu.VMEM((tt, d), F32),
                        pltpu.VMEM((tt, d), F32),
                        pltpu.VMEM((tt, d), F32),
                        pltpu.VMEM(((CONV_W - 1) * V7X_SUBLANES, d), F32),
                        pltpu.VMEM((V7X_SUBLANES, d), F32)],
        compiler_params=_params(("parallel", "arbitrary")),
        name="rglru_layer",
    )(x, scale, shift, gate, w_in, w_out, pm, pmt, conv_w, conv_b, w_ra, b_ra, w_ri, b_ri, lam,
      *[stacked for stacked, _, _ in casts])
    return out[0], out[1:]


def _gate_lanes(ig, fg):
    pad = [(0, 0)] * (ig.ndim - 1) + [(0, V7X_LANES - ig.shape[-1])]
    return jnp.concatenate([jnp.pad(ig, pad), jnp.pad(fg, pad)], axis=-1)


def kernel(x, c, ada_w, ada_b, a_w_in, a_b_gate, a_norm_g, a_w_out, b_w_in, b_conv_w, b_conv_b,
           b_w_ra, b_b_ra, b_w_ri, b_b_ri, b_lam, b_w_out, mlp_w1, mlp_w2, final_g):
    depth = ada_w.shape[0]
    d = x.shape[-1]
    heads = a_b_gate.shape[-1]
    v_w = a_norm_g.shape[-1]
    qk_w = (a_w_in.shape[-1] - 2 * v_w - 2 * heads) // 2
    dqk, dv = qk_w // heads, v_w // heads
    n_main = 2 * qk_w + 2 * v_w

    mod = _ada_mod(c, ada_w, ada_b)
    final_row = final_g.reshape(1, d)

    def site(layer, j):
        m = mod[layer * 2 + j]
        return (m[:, None, d:2 * d], m[:, None, 0:d], m[:, None, 2 * d:3 * d])

    def big_weights(layer):
        slot = layer // 2
        mixer = ((a_w_in, slot, n_main), (a_w_out, slot, d)) if layer % 2 == 0 else \
                ((b_w_in, slot, b_w_in.shape[-1]), (b_w_out, slot, d))
        return mixer + ((mlp_w1, layer, mlp_w1.shape[-1]), (mlp_w2, layer, d))

    first, lead, _ = big_weights(0)[0]
    w_in16 = first[lead].astype(BF16)
    next16 = None
    for layer in range(depth):
        slot = layer // 2
        casts = ()
        if layer == 0:
            casts = big_weights(0)[1:] + (big_weights(1)[:2] if depth > 1 else ())
        elif next16 is not None:
            w_in16, w_out16, *mlp16 = next16
            if mlp16:
                w1_16, w2_16 = mlp16
        scale, shift, gate = site(layer, 0)
        if layer % 2 == 0:
            w_g = a_w_in[slot][:, n_main:]
            w_gates = _gate_lanes(w_g[:, :heads], w_g[:, heads:]).astype(BF16)
            proj, gates, *converted = _norm_matmul(x, scale, shift, w_in16, w_gates, casts=casts,
                                                   n_cols=n_main, tn=n_main // 4)
            gate_bias = _gate_lanes(a_b_gate[slot, 0:1].astype(F32), a_b_gate[slot, 1:2].astype(F32))
            y = _mlstm(proj, gates, gate_bias, a_norm_g[slot].reshape(1, v_w).astype(F32),
                       heads, dqk, dv)
            if converted:
                w_out16, w1_16, w2_16 = converted[:3]
                next16 = converted[3:] or None
            x = _out_proj_residual(y, w_out16, x, gate)
        else:
            assert not casts
            drnn = b_lam.shape[-1]
            x, converted = _rglru_layer(
                x, scale, shift, gate, w_in16, w_out16,
                b_conv_w[slot], b_conv_b[slot].reshape(1, drnn),
                (0.5 * b_w_ra[slot]).astype(BF16), 0.5 * b_b_ra[slot].reshape(1, drnn),
                (0.5 * b_w_ri[slot]).astype(BF16), 0.5 * b_b_ri[slot].reshape(1, drnn),
                b_lam[slot].reshape(1, drnn),
                casts=big_weights(layer)[2:] if layer == 1 else ())
            if converted:
                w1_16, w2_16 = converted
        scale, shift, gate = site(layer, 1)
        last = layer == depth - 1
        host = 1 <= layer < depth - 1
        x, converted = _mlp(x, scale, shift, gate, w1_16, w2_16, final_row, final_norm=last,
                            casts=big_weights(layer + 1) if host else ())
        if converted:
            next16 = converted
    return x
```
